```python
import math
import jax, jax.numpy as jnp
from jax import lax
import numpy as np

D_MODEL = 1024
BATCH = 8
SEQ = 8192
DEPTH = 1
DEC_BATCH = 128
DEC_SEQ = 4
PAST_LEN = 8192
PAGE_SIZE = 128

GLA_HEADS = 4
GLA_DK = 64
GLA_DV = 128
GLA_GATE_RANK = 16
GLA_TAU = 16.0
GLA_CHUNK = 64

MOBA_HEADS = 8
MOBA_HD = 64
MOBA_BLOCK = 256
MOBA_TOPK = 3
MOBA_QCHUNK = 64
ROPE_THETA = 10000.0

N_GROUPS = 4
EXPERTS_PER_GROUP = 4
N_EXPERTS = N_GROUPS * EXPERTS_PER_GROUP
EXPERT_TOPK = 2
D_EXPERT = 512
MOE_ROW_BLOCK = 256
EPS = 1e-6

GLA_QK_W = GLA_HEADS * GLA_DK
GLA_V_W = GLA_HEADS * GLA_DV
MOBA_W = MOBA_HEADS * MOBA_HD
MIX_W = GLA_V_W + MOBA_W
IN_SPLITS = (GLA_QK_W, GLA_QK_W, GLA_V_W, GLA_V_W, GLA_GATE_RANK, MOBA_W, MOBA_W, MOBA_W)
IN_W = sum(IN_SPLITS)

kernel_name = 'hymba_gla_moba_hmoe_step'


def rms_norm(x, g):
    xf = x.astype(jnp.float32)
    y = xf * lax.rsqrt(jnp.mean(xf * xf, axis=-1, keepdims=True) + EPS)
    return (y * g.astype(jnp.float32)).astype(x.dtype)


def rope(x, pos):
    half = x.shape[-1] // 2
    inv = ROPE_THETA ** (-jnp.arange(half, dtype=jnp.float32) / half)
    ang = pos.astype(jnp.float32)[..., None] * inv
    cos = jnp.cos(ang)[..., None, :]
    sin = jnp.sin(ang)[..., None, :]
    xf = x.astype(jnp.float32)
    x1, x2 = xf[..., :half], xf[..., half:]
    return jnp.concatenate([x1 * cos - x2 * sin, x2 * cos + x1 * sin], axis=-1).astype(x.dtype)


def gla_chunked(q, k, v, log_a, s0):
    B, L, H, dk = q.shape
    dv = v.shape[-1]
    C = math.gcd(L, GLA_CHUNK)
    n = L // C

    def to_chunks(t):
        return t.astype(jnp.float32).reshape(B, n, C, H, t.shape[-1]).transpose(1, 0, 3, 2, 4)

    qc, kc, vc, ac = to_chunks(q), to_chunks(k), to_chunks(v), to_chunks(log_a)
    causal = jnp.tril(jnp.ones((C, C), dtype=bool))

    def step(S, inp):
        qi, ki, vi, ai = inp
        b = jnp.cumsum(ai, axis=2)
        diff = b[:, :, :, None, :] - b[:, :, None, :, :]
        decay = jnp.exp(jnp.where(causal[:, :, None], diff, -jnp.inf))
        A = jnp.einsum('bhtd,bhsd,bhtsd->bhts', qi, ki, decay)
        o = jnp.einsum('bhts,bhsv->bhtv', A, vi) + jnp.einsum('bhtd,bhdv->bhtv', qi * jnp.exp(b), S)
        bl = b[:, :, -1, :]
        S = jnp.exp(bl)[..., None] * S + jnp.einsum('bhsd,bhsv->bhdv', ki * jnp.exp(bl[:, :, None, :] - b), vi)
        return S, o

    S, o = lax.scan(step, s0, (qc, kc, vc, ac))
    o = o.transpose(1, 0, 3, 2, 4).reshape(B, L, H, dv)
    return o, S


def moba_attend(q, qpos, kbt, vbt, kmean):
    Q, H, hd = q.shape
    NB = kbt.shape[1]
    qf = q.astype(jnp.float32)
    own = qpos // MOBA_BLOCK

    gate = jnp.einsum('qhd,hnd->qhn', qf, kmean)
    fully_past = jnp.arange(NB)[None, :] < own[:, None]
    gate = jnp.where(fully_past[:, None, :], gate, -jnp.inf)
    if NB < MOBA_TOPK:
        gate = jnp.pad(gate, ((0, 0), (0, 0), (0, MOBA_TOPK - NB)), constant_values=-jnp.inf)
    _, sel = lax.top_k(gate, MOBA_TOPK)
    sel = jnp.minimum(sel, NB - 1)

    blocks = jnp.concatenate([sel, jnp.broadcast_to(own[:, None, None], (Q, H, 1))], axis=-1)
    hidx = jnp.arange(H)[None, :, None]
    ks = kbt[hidx, blocks].astype(jnp.float32)
    vs = vbt[hidx, blocks].astype(jnp.float32)

    rank_ok = jnp.arange(MOBA_TOPK)[None, :] < own[:, None]
    sel_ok = jnp.broadcast_to(rank_ok[:, None, :, None], (Q, H, MOBA_TOPK, MOBA_BLOCK))
    own_kpos = own[:, None] * MOBA_BLOCK + jnp.arange(MOBA_BLOCK)[None, :]
    own_ok = jnp.broadcast_to((own_kpos <= qpos[:, None])[:, None, None, :], (Q, H, 1, MOBA_BLOCK))
    valid = jnp.concatenate([sel_ok, own_ok], axis=2)

    s = jnp.einsum('qhd,qhnkd->qhnk', qf, ks) * (hd ** -0.5)
    s = jnp.where(valid, s, -jnp.inf).reshape(Q, H, -1)
    p = jax.nn.softmax(s, axis=-1)
    o = jnp.einsum('qhm,qhmd->qhd', p, vs.reshape(Q, H, -1, hd))
    return o.astype(q.dtype)


def moba_prompt(q, k, v):
    B, L, H, hd = q.shape
    NB = -(-L // MOBA_BLOCK)
    pad = NB * MOBA_BLOCK - L

    def to_blocks(t):
        t = jnp.pad(t, ((0, 0), (0, pad), (0, 0), (0, 0)))
        return t.reshape(B, NB, MOBA_BLOCK, H, hd).transpose(0, 3, 1, 2, 4)

    kbt, vbt = to_blocks(k), to_blocks(v)
    kmean = jnp.mean(kbt.astype(jnp.float32), axis=3)

    QC = math.gcd(L, MOBA_QCHUNK)
    nq = L // QC
    qc = q.reshape(B * nq, QC, H, hd)
    bidx = jnp.repeat(jnp.arange(B), nq)
    cidx = jnp.tile(jnp.arange(nq), B)

    def one(inp):
        qi, b, c = inp
        return moba_attend(qi, c * QC + jnp.arange(QC), kbt[b], vbt[b], kmean[b])

    return lax.map(one, (qc, bidx, cidx)).reshape(B, L, H, hd)


def moba_sample(q, k_new, v_new, cache_k, cache_v, page_table):
    DB, T, H, hd = q.shape
    past = page_table.shape[1] * PAGE_SIZE
    L = past + T
    NB = -(-L // MOBA_BLOCK)
    pad = NB * MOBA_BLOCK - L
    qpos = past + jnp.arange(T)

    def to_blocks(page_rows, new):
        full = jnp.concatenate([page_rows.reshape(past, H, hd), new], axis=0)
        full = jnp.pad(full, ((0, pad), (0, 0), (0, 0)))
        return full.reshape(NB, MOBA_BLOCK, H, hd).transpose(2, 0, 1, 3)

    def one(inp):
        qi, kn, vn, pt = inp
        kbt = to_blocks(cache_k[pt], kn)
        vbt = to_blocks(cache_v[pt], vn)
        kmean = jnp.mean(kbt.astype(jnp.float32), axis=2)
        return moba_attend(qi, qpos, kbt, vbt, kmean)

    return lax.map(one, (q, k_new, v_new, page_table))


def expert_dispatch(x, eid, wt, w_gate, w_up, w_down):
    N, D = x.shape
    A = eid.shape[0]
    BLK = MOE_ROW_BLOCK
    n_blk = -(-A // BLK) + N_EXPERTS
    R = n_blk * BLK
    tok = jnp.arange(A, dtype=jnp.int32) // EXPERT_TOPK

    counts = jnp.bincount(eid, length=N_EXPERTS)
    padded = (counts + BLK - 1) // BLK * BLK
    pend = jnp.cumsum(padded)
    pstart = pend - padded
    cstart = jnp.cumsum(counts) - counts

    order = jnp.argsort(eid)
    se = eid[order]
    dest = pstart[se] + (jnp.arange(A) - cstart[se])
    row_tok = jnp.full((R,), N, dtype=jnp.int32).at[dest].set(tok[order])
    row_w = jnp.zeros((R,), dtype=jnp.float32).at[dest].set(wt[order])
    blk_expert = jnp.minimum(jnp.searchsorted(pend, jnp.arange(n_blk) * BLK, side='right'), N_EXPERTS - 1)

    xpad = jnp.concatenate([x, jnp.zeros((1, D), dtype=x.dtype)], axis=0)
    xs = xpad[row_tok].reshape(n_blk, BLK, D)

    def ffn(inp):
        xb, e = inp
        h = jax.nn.silu(xb @ w_gate[e]) * (xb @ w_up[e])
        return h @ w_down[e]

    ys = lax.map(ffn, (xs, blk_expert)).reshape(R, D)
    out = jnp.zeros((N + 1, D), dtype=x.dtype).at[row_tok].add((ys * row_w[:, None]).astype(x.dtype))
    return out[:N]


def hier_moe(xn, w_group_router, b_group_router, w_expert_router, b_expert_router, w_gate, w_up, w_down):
    shp = xn.shape
    x = xn.reshape(-1, shp[-1])
    N = x.shape[0]
    rows = jnp.arange(N)

    glog = (x @ w_group_router).astype(jnp.float32) + b_group_router.astype(jnp.float32)
    gp = jax.nn.softmax(glog, axis=-1)
    g_sel = jnp.argmax(glog, axis=-1)
    gw = gp[rows, g_sel][:, None]

    elog = ((x @ w_expert_router).astype(jnp.float32) + b_expert_router.astype(jnp.float32))
    elog = elog.reshape(N, N_GROUPS, EXPERTS_PER_GROUP)[rows, g_sel]
    ep = jax.nn.softmax(elog, axis=-1)
    top_p, top_i = lax.top_k(ep, EXPERT_TOPK)
    top_p = top_p / jnp.sum(top_p, axis=-1, keepdims=True) * gw

    eid = (g_sel[:, None] * EXPERTS_PER_GROUP + top_i).astype(jnp.int32)
    y = expert_dispatch(x, eid.reshape(-1), top_p.reshape(-1), w_gate, w_up, w_down)
    return y.reshape(shp)


def layer_forward(h, pos, gla_s0, moba_fn, g_attn_norm, w_in, w_gla_gate_up, b_gla_gate, g_gla_out, g_q, g_k, w_out,
                  g_ffn_norm, w_group_router, b_group_router, w_expert_router, b_expert_router, w_gate, w_up, w_down):
    B, L, _ = h.shape
    xn = rms_norm(h, g_attn_norm)
    offs = np.cumsum(IN_SPLITS)[:-1].tolist()
    gq, gk, gv, gr, glr, mq, mk, mv = jnp.split(xn @ w_in, offs, axis=-1)

    gq = gq.reshape(B, L, GLA_HEADS, GLA_DK) * (GLA_DK ** -0.5)
    gk = gk.reshape(B, L, GLA_HEADS, GLA_DK)
    gv = gv.reshape(B, L, GLA_HEADS, GLA_DV)
    log_a = jax.nn.log_sigmoid((glr @ w_gla_gate_up + b_gla_gate).astype(jnp.float32)) / GLA_TAU
    log_a = log_a.reshape(B, L, GLA_HEADS, GLA_DK)
    go, s_new = gla_chunked(gq, gk, gv, log_a, gla_s0)
    go = rms_norm(go, g_gla_out).reshape(B, L, GLA_V_W).astype(h.dtype) * jax.nn.silu(gr)

    mq = rope(rms_norm(mq.reshape(B, L, MOBA_HEADS, MOBA_HD), g_q), pos)
    mk = rope(rms_norm(mk.reshape(B, L, MOBA_HEADS, MOBA_HD), g_k), pos)
    mv = mv.reshape(B, L, MOBA_HEADS, MOBA_HD)
    mo = moba_fn(mq, mk, mv).reshape(B, L, MOBA_W).astype(h.dtype)

    h = h + jnp.concatenate([go, mo], axis=-1) @ w_out
    h = h + hier_moe(rms_norm(h, g_ffn_norm), w_group_router, b_group_router, w_expert_router, b_expert_router,
                     w_gate, w_up, w_down)
    return h, mk, mv, s_new


def setup_inputs(seed: int = 0) -> dict:
    key = jax.random.key(seed)
    ks = jax.random.split(key, 24)
    f32 = jnp.float32
    n_pages = PAST_LEN // PAGE_SIZE
    n_phys = (DEC_BATCH * n_pages * 5) // 4

    def nrm(k, shape, scale):
        return jax.random.normal(k, shape, f32) * scale

    def gain(k, shape):
        return 1.0 + 0.02 * jax.random.normal(k, shape, f32)

    page_table = jax.random.permutation(ks[5], n_phys)[:DEC_BATCH * n_pages]
    page_table = page_table.reshape(DEC_BATCH, n_pages).astype(jnp.int32)

    return {
        'x_prompt': nrm(ks[0], (BATCH, SEQ, D_MODEL), 1.0),
        'x_sample': nrm(ks[1], (DEC_BATCH, DEC_SEQ, D_MODEL), 1.0),
        'cache_k': nrm(ks[2], (DEPTH, n_phys, PAGE_SIZE, MOBA_HEADS, MOBA_HD), 1.0),
        'cache_v': nrm(ks[3], (DEPTH, n_phys, PAGE_SIZE, MOBA_HEADS, MOBA_HD), 1.0),
        'state_gla': nrm(ks[4], (DEPTH, DEC_BATCH, GLA_HEADS, GLA_DK, GLA_DV), 2.0),
        'page_table': page_table,
        'g_attn_norm': gain(ks[6], (DEPTH, D_MODEL)),
        'w_in': nrm(ks[7], (DEPTH, D_MODEL, IN_W), D_MODEL ** -0.5),
        'w_gla_gate_up': nrm(ks[8], (DEPTH, GLA_GATE_RANK, GLA_QK_W), GLA_GATE_RANK ** -0.5),
        'b_gla_gate': nrm(ks[9], (DEPTH, GLA_QK_W), 0.1),
        'g_gla_out': gain(ks[10], (DEPTH, GLA_DV)),
        'g_q': gain(ks[11], (DEPTH, MOBA_HD)),
        'g_k': gain(ks[12], (DEPTH, MOBA_HD)),
        'w_out': nrm(ks[13], (DEPTH, MIX_W, D_MODEL), MIX_W ** -0.5),
        'g_ffn_norm': gain(ks[14], (DEPTH, D_MODEL)),
        'w_group_router': nrm(ks[15], (DEPTH, D_MODEL, N_GROUPS), D_MODEL ** -0.5),
        'b_group_router': nrm(ks[16], (DEPTH, N_GROUPS), 0.01),
        'w_expert_router': nrm(ks[17], (DEPTH, D_MODEL, N_EXPERTS), D_MODEL ** -0.5),
        'b_expert_router': nrm(ks[18], (DEPTH, N_EXPERTS), 0.01),
        'w_gate': nrm(ks[19], (DEPTH, N_EXPERTS, D_MODEL, D_EXPERT), D_MODEL ** -0.5),
        'w_up': nrm(ks[20], (DEPTH, N_EXPERTS, D_MODEL, D_EXPERT), D_MODEL ** -0.5),
        'w_down': nrm(ks[21], (DEPTH, N_EXPERTS, D_EXPERT, D_MODEL), D_EXPERT ** -0.5),
    }


def reference(x_prompt, x_sample, cache_k, cache_v, state_gla, page_table, g_attn_norm, w_in, w_gla_gate_up,
              b_gla_gate, g_gla_out, g_q, g_k, w_out, g_ffn_norm, w_group_router, b_group_router, w_expert_router,
              b_expert_router, w_gate, w_up, w_down):
    B, L, _ = x_prompt.shape
    DB, T, _ = x_sample.shape
    past = page_table.shape[1] * PAGE_SIZE
    pos_p = jnp.arange(L, dtype=jnp.int32)[None]
    pos_s = past + jnp.arange(T, dtype=jnp.int32)[None]

    hp, hs = x_prompt, x_sample
    kp_l, vp_l, sp_l, ks_l, vs_l, ss_l = [], [], [], [], [], []
    for l in range(DEPTH):
        params = (g_attn_norm[l], w_in[l], w_gla_gate_up[l], b_gla_gate[l], g_gla_out[l], g_q[l], g_k[l], w_out[l],
                  g_ffn_norm[l], w_group_router[l], b_group_router[l], w_expert_router[l], b_expert_router[l],
                  w_gate[l], w_up[l], w_down[l])

        s0 = jnp.zeros((B, GLA_HEADS, GLA_DK, GLA_DV), dtype=jnp.float32)
        hp, kp, vp, sp = layer_forward(hp, pos_p, s0, moba_prompt, *params)

        ck, cv = cache_k[l], cache_v[l]
        moba_fn = lambda q, k, v, ck=ck, cv=cv: moba_sample(q, k, v, ck, cv, page_table)
        hs, ksmp, vsmp, ssmp = layer_forward(hs, pos_s, state_gla[l].astype(jnp.float32), moba_fn, *params)

        kp_l.append(kp)
        vp_l.append(vp)
        sp_l.append(sp.astype(state_gla.dtype))
        ks_l.append(ksmp)
        vs_l.append(vsmp)
        ss_l.append(ssmp.astype(state_gla.dtype))

    return (hp, hs, jnp.stack(kp_l), jnp.stack(vp_l), jnp.stack(sp_l), jnp.stack(ks_l), jnp.stack(vs_l), jnp.stack(ss_l))
```

```python
import functools
import math

import jax
import jax.numpy as jnp
import numpy as np
from jax import lax
from jax.experimental import pallas as pl
from jax.experimental.pallas import tpu as pltpu

GLA_HEADS = 4
GLA_DK = 64
GLA_DV = 128
GLA_GATE_RANK = 16
GLA_TAU = 16.0
MOBA_HEADS = 8
MOBA_HD = 64
MOBA_BLOCK = 256
MOBA_TOPK = 3
ROPE_THETA = 10000.0
N_GROUPS = 4
EXPERTS_PER_GROUP = 4
N_EXPERTS = N_GROUPS * EXPERTS_PER_GROUP
EXPERT_TOPK = 2
EPS = 1e-6

GLA_QK_W = GLA_HEADS * GLA_DK
GLA_V_W = GLA_HEADS * GLA_DV
MOBA_W = MOBA_HEADS * MOBA_HD

LANES = 128
SUBLANES = 8
VMEM_LIMIT_BYTES = 56 * 1024 * 1024

ROW_TILE = 256
GLA_CHUNK = 256
MOE_BLOCK = 256
SAMPLE_PAGES_PER_STEP = 8

F32 = jnp.float32
BF16 = jnp.bfloat16
NEG = -1e30


def _dot(a, b):
    return jnp.dot(a, b, preferred_element_type=F32)


def _dot_nt(a, b):
    return lax.dot_general(a, b, (((1,), (1,)), ((), ())), preferred_element_type=F32)


def _dot_tn(a, b):
    return lax.dot_general(a, b, (((0,), (0,)), ((), ())), preferred_element_type=F32)


def _split2(x):
    hi = x.astype(BF16)
    lo = (x - hi.astype(F32)).astype(BF16)
    return hi, lo


def _params(n_axes):
    return pltpu.CompilerParams(dimension_semantics=("arbitrary",) * n_axes,
                                vmem_limit_bytes=VMEM_LIMIT_BYTES)


def _head_norm(y, g):
    lo = lax.broadcasted_iota(jnp.int32, (1, LANES), 1) < MOBA_HD
    outs = []
    for p in range(MOBA_W // LANES):
        yp = y[:, p * LANES:(p + 1) * LANES]
        sq = yp * yp
        s_lo = jnp.sum(jnp.where(lo, sq, 0.0), axis=-1, keepdims=True)
        s_hi = jnp.sum(jnp.where(lo, 0.0, sq), axis=-1, keepdims=True)
        ms = jnp.where(lo, s_lo, s_hi) * (1.0 / MOBA_HD)
        outs.append(yp * lax.rsqrt(ms + EPS))
    return jnp.concatenate(outs, axis=1) * g


def _rope(y, cos, sin_signed):
    n = y.shape[1]
    first = (lax.broadcasted_iota(jnp.int32, (1, n), 1) % MOBA_HD) < (MOBA_HD // 2)
    rot = jnp.where(first, pltpu.roll(y, n - MOBA_HD // 2, 1), pltpu.roll(y, MOBA_HD // 2, 1))
    return y * cos + rot * sin_signed


def _inproj_kernel(x_ref, gin_ref, wg_ref, wlr_ref, wup_ref, bgate_ref, wm_ref, gq_ref, gk_ref, cos_ref, sin_ref,
                   oq_ref, ok_ref, ov_ref, or_ref, ola_ref, omq_ref, omk_ref, omv_ref):
    x = x_ref[...]
    ms = jnp.mean(x * x, axis=-1, keepdims=True)
    xn = (x * lax.rsqrt(ms + EPS) * gin_ref[...]).astype(BF16)

    yg = _dot(xn, wg_ref[...])
    oq_ref[...] = yg[:, :GLA_QK_W] * (GLA_DK ** -0.5)
    ok_ref[...] = yg[:, GLA_QK_W:2 * GLA_QK_W]
    ov_ref[...] = yg[:, 2 * GLA_QK_W:2 * GLA_QK_W + GLA_V_W]
    or_ref[...] = yg[:, 2 * GLA_QK_W + GLA_V_W:]

    lr = _dot(xn, wlr_ref[...])
    lr_hi, lr_lo = _split2(lr)
    wu_hi = wup_ref[0]
    wu_lo = wup_ref[1]
    z = _dot(lr_hi, wu_hi) + _dot(lr_lo, wu_hi) + _dot(lr_hi, wu_lo) + bgate_ref[...]
    ola_ref[...] = (jnp.minimum(z, 0.0) - jnp.log(1.0 + jnp.exp(-jnp.abs(z)))) * (1.0 / GLA_TAU)

    ym = _dot(xn, wm_ref[...])
    reps = MOBA_W // LANES
    cos = jnp.concatenate([cos_ref[...]] * reps, axis=1)
    sin = jnp.concatenate([sin_ref[...]] * reps, axis=1)
    omq_ref[...] = _rope(_head_norm(ym[:, :MOBA_W], gq_ref[...]), cos, sin)
    omk_ref[...] = _rope(_head_norm(ym[:, MOBA_W:2 * MOBA_W], gk_ref[...]), cos, sin)
    omv_ref[...] = ym[:, 2 * MOBA_W:]


def _inproj(x2d, w, cos_tab, sin_tab):
    n, d = x2d.shape
    t = ROW_TILE
    assert n % t == 0 and cos_tab.shape[0] % t == 0
    n_pos_tiles = cos_tab.shape[0] // t
    row = lambda i: (i, 0)
    fixed = lambda i: (0, 0)
    widths = (GLA_QK_W, GLA_QK_W, GLA_V_W, GLA_V_W, GLA_QK_W, MOBA_W, MOBA_W, MOBA_W)
    return pl.pallas_call(
        _inproj_kernel,
        grid=(n // t,),
        in_specs=[
            pl.BlockSpec((t, d), row),
            pl.BlockSpec((1, d), fixed),
            pl.BlockSpec(w["wg"].shape, fixed),
            pl.BlockSpec(w["wlr"].shape, fixed),
            pl.BlockSpec(w["wup"].shape, lambda i: (0, 0, 0)),
            pl.BlockSpec((1, GLA_QK_W), fixed),
            pl.BlockSpec(w["wm"].shape, fixed),
            pl.BlockSpec((1, MOBA_W), fixed),
            pl.BlockSpec((1, MOBA_W), fixed),
            pl.BlockSpec((t, LANES), lambda i: (i % n_pos_tiles, 0)),
            pl.BlockSpec((t, LANES), lambda i: (i % n_pos_tiles, 0)),
        ],
        out_specs=[pl.BlockSpec((t, wd), row) for wd in widths],
        out_shape=[jax.ShapeDtypeStruct((n, wd), F32) for wd in widths],
        compiler_params=_params(1),
        name="inproj",
    )(x2d, w["g_attn"], w["wg"], w["wlr"], w["wup"], w["b_gate"], w["wm"], w["g_q"], w["g_k"], cos_tab, sin_tab)


def _rope_tables(positions):
    half = MOBA_HD // 2
    inv = ROPE_THETA ** (-np.arange(half, dtype=np.float64) / half)
    ang = positions.astype(np.float64)[:, None] * inv[None, :]
    cos = np.cos(ang)
    sin = np.sin(ang)
    cos_h = np.concatenate([cos, cos], axis=1)
    sin_h = np.concatenate([-sin, sin], axis=1)
    reps = LANES // MOBA_HD
    return (jnp.asarray(np.tile(cos_h, (1, reps)), F32), jnp.asarray(np.tile(sin_h, (1, reps)), F32))


def _gla_level_matrix(c):
    t = np.arange(c)[:, None]
    j = np.arange(c)[None, :]
    mats = [(j <= t)]
    m = c // 2
    while m >= 1:
        base = (t // (2 * m)) * (2 * m)
        ref = base + m - 1
        upper = (t - base) >= m
        mats.append(np.where(upper, (j > ref) & (j <= t), (j > t) & (j <= ref)))
        m //= 2
    return jnp.asarray(np.concatenate(mats, axis=0).astype(np.float32), BF16)


def _gla_kernel(q_ref, k_ref, v_ref, la_ref, s0_ref, lvl_ref, o_ref, sout_ref, s_ref, *, c):
    ci = pl.program_id(1)
    n_lev = int(math.log2(c))
    pair_w = 2 * GLA_DK

    @pl.when(ci == 0)
    def _():
        s_ref[...] = s0_ref[...]

    la = la_ref[...]
    la_hi, la_lo = _split2(la)
    lvl = lvl_ref[...]
    x_all = _dot(lvl, la_hi) + _dot(lvl, la_lo)
    b = x_all[0:c]
    b_last = b[c - 1:c, :]
    q = q_ref[...]
    k = k_ref[...]
    qe = q * jnp.exp(b)
    kd = (k * jnp.exp(b_last - b)).astype(BF16)

    ti = lax.broadcasted_iota(jnp.int32, (c, c), 0)
    si = lax.broadcasted_iota(jnp.int32, (c, c), 1)
    trow = lax.broadcasted_iota(jnp.int32, (c, 1), 0)
    lane = lax.broadcasted_iota(jnp.int32, (1, pair_w), 1)
    ones_cw = jnp.ones((c, pair_w), BF16)
    srow = lax.broadcasted_iota(jnp.int32, (pair_w, 1), 0)

    for p in range(GLA_HEADS // 2):
        cols = slice(p * pair_w, (p + 1) * pair_w)
        q_p = q[:, cols]
        k_p = k[:, cols]
        qe_p = qe[:, cols]
        head_lanes = (lane < GLA_DK, lane >= GLA_DK)
        s_pair = s_ref[cols, :]
        s_pair_b = s_pair.astype(BF16)

        a = [jnp.where(ti == si, _dot_nt(jnp.where(hm, q_p, 0.0).astype(BF16), k_p.astype(BF16)), 0.0)
             for hm in head_lanes]
        for lev in range(n_lev):
            m_log = n_lev - 1 - lev
            e = jnp.exp(x_all[(lev + 1) * c:(lev + 2) * c, cols])
            upper = ((trow >> m_log) & 1) == 1
            kt = jnp.where(upper, 0.0, k_p * e).astype(BF16)
            qt = jnp.where(upper, q_p * e, 0.0)
            same = (ti >> (m_log + 1)) == (si >> (m_log + 1))
            for hh in range(2):
                al = _dot_nt(jnp.where(head_lanes[hh], qt, 0.0).astype(BF16), kt)
                a[hh] = a[hh] + jnp.where(same, al, 0.0)

        u = []
        for hh in range(2):
            h = 2 * p + hh
            v_h = v_ref[:, h * GLA_DV:(h + 1) * GLA_DV].astype(BF16)
            o_h = _dot(a[hh].astype(BF16), v_h) + _dot(jnp.where(head_lanes[hh], qe_p, 0.0).astype(BF16), s_pair_b)
            o_ref[:, h * GLA_DV:(h + 1) * GLA_DV] = o_h
            u.append(_dot_tn(kd[:, cols], v_h))
        bl_rows = _dot_tn(la_hi[:, cols], ones_cw) + _dot_tn(la_lo[:, cols], ones_cw)
        s_ref[cols, :] = jnp.exp(bl_rows[:, :GLA_DV]) * s_pair + jnp.where(srow < GLA_DK, u[0], u[1])

    @pl.when(ci == pl.num_programs(1) - 1)
    def _():
        sout_ref[...] = s_ref[...]


def _gla(q, k, v, la, s0, c):
    bsz, l, _ = q.shape
    assert l % c == 0 and (c & (c - 1)) == 0 and GLA_DV == 2 * GLA_DK
    lvl = _gla_level_matrix(c)
    tok = lambda b, i: (b, i, 0)
    per_b = lambda b, i: (b, 0, 0)
    return pl.pallas_call(
        functools.partial(_gla_kernel, c=c),
        grid=(bsz, l // c),
        in_specs=[
            pl.BlockSpec((None, c, GLA_QK_W), tok),
            pl.BlockSpec((None, c, GLA_QK_W), tok),
            pl.BlockSpec((None, c, GLA_V_W), tok),
            pl.BlockSpec((None, c, GLA_QK_W), tok),
            pl.BlockSpec((None, GLA_QK_W, GLA_DV), per_b),
            pl.BlockSpec(lvl.shape, lambda b, i: (0, 0)),
        ],
        out_specs=[pl.BlockSpec((None, c, GLA_V_W), tok), pl.BlockSpec((None, GLA_QK_W, GLA_DV), per_b)],
        out_shape=[jax.ShapeDtypeStruct((bsz, l, GLA_V_W), F32), jax.ShapeDtypeStruct((bsz, GLA_QK_W, GLA_DV), F32)],
        scratch_shapes=[pltpu.VMEM((GLA_QK_W, GLA_DV), F32)],
        compiler_params=_params(2),
        name="gla",
    )(q, k, v, la, s0, lvl)


def _select_topk_blocks(gate, n_valid):
    row = lax.broadcasted_iota(jnp.int32, gate.shape, 0)
    g = jnp.where(row < n_valid, gate, NEG)
    sel = jnp.zeros(gate.shape, F32)
    for r in range(MOBA_TOPK):
        mx = jnp.max(g, axis=0, keepdims=True)
        idx = jnp.min(jnp.where(g == mx, row, gate.shape[0]), axis=0, keepdims=True)
        pick = row == jnp.where(r < n_valid, idx, -1)
        sel = jnp.where(pick, 1.0, sel)
        g = jnp.where(pick, NEG, g)
    return sel


def _moba_prompt_kernel(q_ref, k_ref, v_ref, o_ref, kmean_ref, sel_ref, *, nb):
    h = pl.program_id(1)
    qi = pl.program_id(2)
    blk = MOBA_BLOCK

    @pl.when(qi == 0)
    def _():
        kmean_ref[...] = jnp.zeros(kmean_ref.shape, F32)

        def body(n, carry):
            kb = k_ref[n].astype(F32)
            kmean_ref[pl.ds(n, 1), :] = jnp.sum(kb, axis=0, keepdims=True) * (1.0 / blk)
            return carry

        lax.fori_loop(0, nb, body, 0)

    q = q_ref[...]
    row = lax.broadcasted_iota(jnp.int32, (2 * MOBA_HD, blk), 0)
    mine = (row // MOBA_HD) == (h % 2)
    q_pair = jnp.where(mine, jnp.concatenate([q, q], axis=0), jnp.zeros((), BF16))

    km_hi, km_lo = _split2(kmean_ref[...])
    gate = _dot(km_hi, q_pair) + _dot(km_lo, q_pair)
    sel_ref[...] = _select_topk_blocks(gate, qi)

    def update(carry, s, v_t):
        m, l, acc = carry
        m_new = jnp.maximum(m, jnp.max(s, axis=0, keepdims=True))
        alpha = jnp.exp(m - m_new)
        p = jnp.exp(s - m_new)
        l = l * alpha + jnp.sum(p, axis=0, keepdims=True)
        acc = acc * alpha + _dot(v_t, p.astype(BF16))
        return m_new, l, acc

    def past_block(n, carry):
        s = _dot(k_ref[n], q_pair)
        s = jnp.where(sel_ref[pl.ds(n, 1), :] > 0.5, s, NEG)
        return update(carry, s, v_ref[n])

    carry = (jnp.full((1, blk), NEG, F32), jnp.zeros((1, blk), F32), jnp.zeros((MOBA_HD, blk), F32))
    carry = lax.fori_loop(0, qi, past_block, carry)

    s = _dot(k_ref[qi], q_pair)
    causal = lax.broadcasted_iota(jnp.int32, (blk, blk), 0) <= lax.broadcasted_iota(jnp.int32, (blk, blk), 1)
    _, l, acc = update(carry, jnp.where(causal, s, NEG), v_ref[qi])
    o_ref[...] = jnp.transpose(acc / l)


def _moba_prompt(mq, mk, mv):
    bsz, l, _ = mq.shape
    blk = MOBA_BLOCK
    assert l % blk == 0
    nb = l // blk
    nb_pad = -(-nb // SUBLANES) * SUBLANES
    hp = MOBA_HEADS // 2
    qt = (mq * (MOBA_HD ** -0.5)).astype(BF16).reshape(bsz, nb, blk, MOBA_HEADS, MOBA_HD).transpose(0, 3, 1, 4, 2)
    vt = mv.astype(BF16).reshape(bsz, nb, blk, MOBA_HEADS, MOBA_HD).transpose(0, 3, 1, 4, 2)
    kp = mk.astype(BF16).reshape(bsz, nb, blk, hp, 2 * MOBA_HD).transpose(0, 3, 1, 2, 4)
    out = pl.pallas_call(
        functools.partial(_moba_prompt_kernel, nb=nb),
        grid=(bsz, MOBA_HEADS, nb),
        in_specs=[
            pl.BlockSpec((None, None, None, MOBA_HD, blk), lambda b, h, i: (b, h, i, 0, 0)),
            pl.BlockSpec((None, None, nb, blk, 2 * MOBA_HD), lambda b, h, i: (b, h // 2, 0, 0, 0)),
            pl.BlockSpec((None, None, nb, MOBA_HD, blk), lambda b, h, i: (b, h, 0, 0, 0)),
        ],
        out_specs=pl.BlockSpec((None, None, blk, MOBA_HD), lambda b, h, i: (b, h, i, 0)),
        out_shape=jax.ShapeDtypeStruct((bsz, MOBA_HEADS, l, MOBA_HD), F32),
        scratch_shapes=[pltpu.VMEM((nb_pad, 2 * MOBA_HD), F32), pltpu.VMEM((nb_pad, blk), F32)],
        compiler_params=_params(3),
        name="moba_prompt",
    )(qt, kp, vt)
    return out.transpose(0, 2, 1, 3).reshape(bsz, l, MOBA_W)


def _moba_sample_kernel(pt_ref, qb_ref, kn_ref, vn_ref, *refs, npp, n_steps, n_pages, page, t_new):
    k_refs = refs[:npp]
    v_refs = refs[npp:2 * npp]
    o_ref = refs[2 * npp]
    s_ref, kmean_ref, sel_ref, pt_t_ref, acc_ref = refs[2 * npp + 1:]
    j = pl.program_id(1)
    ppb = MOBA_BLOCK // page
    n_blk = n_pages // ppb
    n_cols = LANES
    qb = qb_ref[...]

    def page_rows(pg):
        return pl.ds(pl.multiple_of(pg * page, page), page)

    @pl.when(j < n_steps)
    def _():
        prev = None
        for i in range(npp):
            kp = k_refs[i][...]
            s_ref[page_rows(j * npp + i), :] = _dot(kp.astype(BF16), qb)
            cs = jnp.sum(kp, axis=0, keepdims=True)
            prev = cs if i % ppb == 0 else prev + cs
            if i % ppb == ppb - 1:
                kmean_ref[pl.ds(j * (npp // ppb) + i // ppb, 1), :] = prev * (1.0 / MOBA_BLOCK)

    @pl.when(j == n_steps - 1)
    def _():
        km_hi, km_lo = _split2(kmean_ref[...])
        gate = _dot(km_hi, qb) + _dot(km_lo, qb)
        sel_ref[...] = _select_topk_blocks(gate, n_blk)

        s_own = _dot(kn_ref[...].astype(BF16), qb)
        key_t = lax.broadcasted_iota(jnp.int32, (page, n_cols), 0)
        col_t = lax.broadcasted_iota(jnp.int32, (page, n_cols), 1) // MOBA_HEADS
        s_own = jnp.where((key_t < t_new) & (key_t <= col_t), s_own, NEG)
        s_ref[page_rows(n_pages), :] = s_own

        def max_body(pg, m):
            s = s_ref[page_rows(pg), :]
            s = jnp.where(sel_ref[pl.ds(pg // ppb, 1), :] > 0.5, s, NEG)
            return jnp.maximum(m, jnp.max(s, axis=0, keepdims=True))

        m = lax.fori_loop(0, n_pages, max_body, jnp.max(s_own, axis=0, keepdims=True))

        def exp_body(pg, l):
            s = s_ref[page_rows(pg), :]
            e = jnp.where(sel_ref[pl.ds(pg // ppb, 1), :] > 0.5, jnp.exp(s - m), 0.0)
            s_ref[page_rows(pg), :] = e
            return l + jnp.sum(e, axis=0, keepdims=True)

        e_own = jnp.exp(s_own - m)
        s_ref[page_rows(n_pages), :] = e_own
        l = lax.fori_loop(0, n_pages, exp_body, jnp.sum(e_own, axis=0, keepdims=True))
        inv = 1.0 / l

        def tr_body(pg, carry):
            pt_t_ref[:, page_rows(pg)] = jnp.transpose(s_ref[page_rows(pg), :] * inv).astype(BF16)
            return carry

        lax.fori_loop(0, n_pages + 1, tr_body, 0)
        acc_ref[...] = jnp.zeros(acc_ref.shape, F32)

    rows_used = acc_ref.shape[0]

    @pl.when(j >= n_steps)
    def _():
        for i in range(npp):
            pg = (j - n_steps) * npp + i
            acc_ref[...] += _dot(pt_t_ref[0:rows_used, page_rows(pg)], v_refs[i][...].astype(BF16))

    @pl.when(j == 2 * n_steps - 1)
    def _():
        o_all = acc_ref[...] + _dot(pt_t_ref[0:rows_used, page_rows(n_pages)], vn_ref[...].astype(BF16))
        row_h = lax.broadcasted_iota(jnp.int32, o_all.shape, 0) % MOBA_HEADS
        col_h = lax.broadcasted_iota(jnp.int32, o_all.shape, 1) // MOBA_HD
        o_all = jnp.where(row_h == col_h, o_all, 0.0)
        for t in range(t_new):
            o_ref[t:t + 1, :] = jnp.sum(o_all[t * MOBA_HEADS:(t + 1) * MOBA_HEADS], axis=0, keepdims=True)


def _moba_sample(mq, mk, mv, cache_k, cache_v, page_table):
    dbs, t_new, _ = mq.shape
    _, page, _ = cache_k.shape
    n_pages = page_table.shape[1]
    npp = SAMPLE_PAGES_PER_STEP
    ppb = MOBA_BLOCK // page
    assert MOBA_BLOCK % page == 0 and npp % ppb == 0 and n_pages % npp == 0
    assert (n_pages * page) % MOBA_BLOCK == 0 and t_new <= page and t_new * MOBA_HEADS <= LANES
    n_steps = n_pages // npp
    n_blk = n_pages // ppb
    rows_used = -(-t_new * MOBA_HEADS // 16) * 16

    q4 = (mq * (MOBA_HD ** -0.5)).reshape(dbs, t_new, MOBA_HEADS, MOBA_HD)
    eye = jnp.eye(MOBA_HEADS, dtype=F32)
    qb = jnp.einsum("bthd,hg->bhdtg", q4, eye).reshape(dbs, MOBA_W, t_new * MOBA_HEADS)
    qb = jnp.pad(qb, ((0, 0), (0, 0), (0, LANES - t_new * MOBA_HEADS))).astype(BF16)
    kn = jnp.pad(mk, ((0, 0), (0, page - t_new), (0, 0)))
    vn = jnp.pad(mv, ((0, 0), (0, page - t_new), (0, 0)))
    pt_flat = page_table.reshape(-1).astype(jnp.int32)

    def k_map(i):
        return lambda b, j, pt: (pt[b * n_pages + jnp.minimum(j, n_steps - 1) * npp + i], 0, 0)

    def v_map(i):
        return lambda b, j, pt: (pt[b * n_pages + jnp.maximum(j - n_steps, 0) * npp + i], 0, 0)

    per_b = lambda b, j, pt: (b, 0, 0)
    grid_spec = pltpu.PrefetchScalarGridSpec(
        num_scalar_prefetch=1,
        grid=(dbs, 2 * n_steps),
        in_specs=[pl.BlockSpec((None, MOBA_W, LANES), per_b),
                  pl.BlockSpec((None, page, MOBA_W), per_b),
                  pl.BlockSpec((None, page, MOBA_W), per_b)]
                 + [pl.BlockSpec((None, page, MOBA_W), k_map(i)) for i in range(npp)]
                 + [pl.BlockSpec((None, page, MOBA_W), v_map(i)) for i in range(npp)],
        out_specs=pl.BlockSpec((None, t_new, MOBA_W), per_b),
        scratch_shapes=[
            pltpu.VMEM(((n_pages + 1) * page, LANES), F32),
            pltpu.VMEM((n_blk, MOBA_W), F32),
            pltpu.VMEM((n_blk, LANES), F32),
            pltpu.VMEM((LANES, (n_pages + 1) * page), BF16),
            pltpu.VMEM((rows_used, MOBA_W), F32),
        ],
    )
    return pl.pallas_call(
        functools.partial(_moba_sample_kernel, npp=npp, n_steps=n_steps, n_pages=n_pages, page=page, t_new=t_new),
        grid_spec=grid_spec,
        out_shape=jax.ShapeDtypeStruct((dbs, t_new, MOBA_W), F32),
        compiler_params=_params(2),
        name="moba_sample",
    )(pt_flat, qb, kn, vn, *([cache_k] * npp), *([cache_v] * npp))


def _outproj_kernel(x_ref, go_ref, gr_ref, mo_ref, ggo_ref, wo1_ref, wo2_ref, gffn_ref, wr_ref, br_ref,
                    h_ref, xn_ref, eid_ref, wt_ref, rank_ref, cnt_ref, carry_ref):
    i = pl.program_id(0)
    t = x_ref.shape[0]

    @pl.when(i == 0)
    def _():
        carry_ref[...] = jnp.zeros(carry_ref.shape, F32)

    gr = gr_ref[...]
    gate = gr / (1.0 + jnp.exp(-gr))
    parts = []
    for hd in range(GLA_HEADS):
        cols = slice(hd * GLA_DV, (hd + 1) * GLA_DV)
        o = go_ref[:, cols]
        ms = jnp.mean(o * o, axis=-1, keepdims=True)
        parts.append(o * lax.rsqrt(ms + EPS) * ggo_ref[...] * gate[:, cols])
    a1 = jnp.concatenate(parts, axis=1).astype(BF16)
    h = x_ref[...] + _dot(a1, wo1_ref[...]) + _dot(mo_ref[...].astype(BF16), wo2_ref[...])
    h_ref[...] = h

    ms = jnp.mean(h * h, axis=-1, keepdims=True)
    xn = h * lax.rsqrt(ms + EPS) * gffn_ref[...]
    xn_ref[...] = xn
    logits = _dot_nt(wr_ref[...], xn.astype(BF16)) + br_ref[...]

    gl = logits[0:N_GROUPS]
    grow = lax.broadcasted_iota(jnp.int32, gl.shape, 0)
    gmax = jnp.max(gl, axis=0, keepdims=True)
    g_sel = jnp.min(jnp.where(gl == gmax, grow, N_GROUPS), axis=0, keepdims=True)
    gw = 1.0 / jnp.sum(jnp.exp(gl - gmax), axis=0, keepdims=True)

    el = logits[SUBLANES:SUBLANES + N_EXPERTS]
    erow = lax.broadcasted_iota(jnp.int32, el.shape, 0)
    in_grp = (erow // EXPERTS_PER_GROUP) == g_sel
    emax = jnp.max(jnp.where(in_grp, el, NEG), axis=0, keepdims=True)
    ee = jnp.where(in_grp, jnp.exp(el - emax), 0.0)
    ep = ee / jnp.sum(ee, axis=0, keepdims=True)
    cand = jnp.where(in_grp, ep, -1.0)
    p1 = jnp.max(cand, axis=0, keepdims=True)
    i1 = jnp.min(jnp.where(cand == p1, erow, N_EXPERTS), axis=0, keepdims=True)
    cand = jnp.where(erow == i1, -1.0, cand)
    p2 = jnp.max(cand, axis=0, keepdims=True)
    i2 = jnp.min(jnp.where(cand == p2, erow, N_EXPERTS), axis=0, keepdims=True)
    denom = p1 + p2
    zero_i = jnp.zeros((SUBLANES - EXPERT_TOPK, t), jnp.int32)
    zero_f = jnp.zeros((SUBLANES - EXPERT_TOPK, t), F32)
    eid_ref[...] = jnp.concatenate([i1, i2, zero_i], axis=0)
    wt_ref[...] = jnp.concatenate([p1 / denom * gw, p2 / denom * gw, zero_f], axis=0)

    oh1 = erow == i1
    oh2 = erow == i2
    hits = jnp.where(oh1 | oh2, 1.0, 0.0)
    before = (lax.broadcasted_iota(jnp.int32, (t, t), 0) < lax.broadcasted_iota(jnp.int32, (t, t), 1))
    prefix = _dot(hits.astype(BF16), jnp.where(before, 1.0, 0.0).astype(BF16)) + carry_ref[...]
    r1 = jnp.sum(jnp.where(oh1, prefix, 0.0), axis=0, keepdims=True)
    r2 = jnp.sum(jnp.where(oh2, prefix, 0.0), axis=0, keepdims=True)
    rank_ref[...] = jnp.concatenate([r1.astype(jnp.int32), r2.astype(jnp.int32), zero_i], axis=0)
    carry_ref[...] += jnp.sum(hits, axis=1, keepdims=True)
    cnt_ref[...] = jnp.broadcast_to(carry_ref[...], cnt_ref.shape)


def _outproj(x2d, go, gr, mo, w):
    n, d = x2d.shape
    t = ROW_TILE
    assert n % t == 0
    row = lambda i: (i, 0)
    col = lambda i: (0, i)
    fixed = lambda i: (0, 0)
    return pl.pallas_call(
        _outproj_kernel,
        grid=(n // t,),
        in_specs=[
            pl.BlockSpec((t, d), row),
            pl.BlockSpec((t, GLA_V_W), row),
            pl.BlockSpec((t, GLA_V_W), row),
            pl.BlockSpec((t, MOBA_W), row),
            pl.BlockSpec((1, GLA_DV), fixed),
            pl.BlockSpec(w["wo1"].shape, fixed),
            pl.BlockSpec(w["wo2"].shape, fixed),
            pl.BlockSpec((1, d), fixed),
            pl.BlockSpec(w["wr"].shape, fixed),
            pl.BlockSpec(w["br"].shape, fixed),
        ],
        out_specs=[pl.BlockSpec((t, d), row), pl.BlockSpec((t, d), row),
                   pl.BlockSpec((SUBLANES, t), col), pl.BlockSpec((SUBLANES, t), col),
                   pl.BlockSpec((SUBLANES, t), col), pl.BlockSpec((N_EXPERTS, LANES), fixed)],
        out_shape=[jax.ShapeDtypeStruct((n, d), F32), jax.ShapeDtypeStruct((n, d), F32),
                   jax.ShapeDtypeStruct((SUBLANES, n), jnp.int32), jax.ShapeDtypeStruct((SUBLANES, n), F32),
                   jax.ShapeDtypeStruct((SUBLANES, n), jnp.int32), jax.ShapeDtypeStruct((N_EXPERTS, LANES), F32)],
        scratch_shapes=[pltpu.VMEM((N_EXPERTS, 1), F32)],
        compiler_params=_params(1),
        name="outproj",
    )(x2d, go, gr, mo, w["g_gla_out"], w["wo1"], w["wo2"], w["g_ffn"], w["wr"], w["br"])


def _row_copy(src_ref, src_row, dst_ref, dst_row, sem):
    return pltpu.make_async_copy(src_ref.at[pl.ds(src_row, 1)], dst_ref.at[pl.ds(dst_row, 1)], sem)


def _moe_scatter_kernel(dest_ref, x_ref, xs_in_ref, xs_ref, sem):
    del xs_in_ref
    t = x_ref.shape[0]

    def start(r, carry):
        for kk in range(EXPERT_TOPK):
            _row_copy(x_ref, r, xs_ref, dest_ref[kk, r], sem).start()
        return carry

    def wait(r, carry):
        for kk in range(EXPERT_TOPK):
            _row_copy(x_ref, 0, xs_ref, 0, sem).wait()
        return carry

    lax.fori_loop(0, t, start, 0)
    lax.fori_loop(0, t, wait, 0)


def _moe_scatter(xn, dest, n_rows):
    n, d = xn.shape
    t = ROW_TILE
    xs0 = jnp.zeros((n_rows, d), F32)
    return pl.pallas_call(
        _moe_scatter_kernel,
        grid=(n // t,),
        in_specs=[
            pl.BlockSpec((EXPERT_TOPK, t), lambda i: (0, i), memory_space=pltpu.SMEM),
            pl.BlockSpec((t, d), lambda i: (i, 0)),
            pl.BlockSpec(memory_space=pl.ANY),
        ],
        out_specs=pl.BlockSpec(memory_space=pl.ANY),
        out_shape=jax.ShapeDtypeStruct((n_rows, d), F32),
        scratch_shapes=[pltpu.SemaphoreType.DMA(())],
        input_output_aliases={2: 0},
        compiler_params=_params(1),
        name="moe_scatter",
    )(dest, xn, xs0)


def _moe_ffn_kernel(be_ref, nu_ref, xs_ref, wg_ref, wu_ref, wd_ref, ys_ref):
    i = pl.program_id(0)

    @pl.when(i < nu_ref[0])
    def _():
        x = xs_ref[...].astype(BF16)
        g = _dot(x, wg_ref[...])
        u = _dot(x, wu_ref[...])
        hmid = (g / (1.0 + jnp.exp(-g)) * u).astype(BF16)
        ys_ref[...] = _dot(hmid, wd_ref[...])

    @pl.when(i >= nu_ref[0])
    def _():
        ys_ref[...] = jnp.zeros(ys_ref.shape, F32)


def _moe_ffn(xs, blk_expert, n_used, wg, wu, wd):
    r, d = xs.shape
    de = wg.shape[2]
    n_blk = r // MOE_BLOCK
    rows = lambda i, be, nu: (jnp.minimum(i, nu[0] - 1), 0)
    wsel = lambda i, be, nu: (be[jnp.minimum(i, nu[0] - 1)], 0, 0)
    grid_spec = pltpu.PrefetchScalarGridSpec(
        num_scalar_prefetch=2,
        grid=(n_blk,),
        in_specs=[
            pl.BlockSpec((MOE_BLOCK, d), rows),
            pl.BlockSpec((None, d, de), wsel),
            pl.BlockSpec((None, d, de), wsel),
            pl.BlockSpec((None, de, d), wsel),
        ],
        out_specs=pl.BlockSpec((MOE_BLOCK, d), lambda i, be, nu: (i, 0)),
    )
    return pl.pallas_call(
        _moe_ffn_kernel,
        grid_spec=grid_spec,
        out_shape=jax.ShapeDtypeStruct((r, d), F32),
        compiler_params=_params(1),
        name="moe_ffn",
    )(blk_expert, n_used, xs, wg, wu, wd)


def _moe_combine_kernel(dest_ref, h_ref, wt_ref, ys_ref, y_ref, buf_ref, sem):
    t = h_ref.shape[0]

    def start(r, carry):
        for kk in range(EXPERT_TOPK):
            _row_copy(ys_ref, dest_ref[kk, r], buf_ref.at[kk], r, sem).start()
        return carry

    def wait(r, carry):
        for kk in range(EXPERT_TOPK):
            _row_copy(ys_ref, 0, buf_ref.at[kk], 0, sem).wait()
        return carry

    lax.fori_loop(0, t, start, 0)
    lax.fori_loop(0, t, wait, 0)
    y = h_ref[...]
    for kk in range(EXPERT_TOPK):
        y = y + buf_ref[kk] * wt_ref[:, kk:kk + 1]
    y_ref[...] = y


def _moe_combine(h, wt_rows, dest, ys):
    n, d = h.shape
    t = ROW_TILE
    return pl.pallas_call(
        _moe_combine_kernel,
        grid=(n // t,),
        in_specs=[
            pl.BlockSpec((EXPERT_TOPK, t), lambda i: (0, i), memory_space=pltpu.SMEM),
            pl.BlockSpec((t, d), lambda i: (i, 0)),
            pl.BlockSpec((t, SUBLANES), lambda i: (i, 0)),
            pl.BlockSpec(memory_space=pl.ANY),
        ],
        out_specs=pl.BlockSpec((t, d), lambda i: (i, 0)),
        out_shape=jax.ShapeDtypeStruct((n, d), F32),
        scratch_shapes=[pltpu.VMEM((EXPERT_TOPK, t, d), F32), pltpu.SemaphoreType.DMA(())],
        compiler_params=_params(1),
        name="moe_combine",
    )(dest, h, wt_rows, ys)


def _moe(h, xn, eid, wt, rank, counts, w):
    n, d = h.shape
    blk = MOE_BLOCK
    n_blk = -(-(n * EXPERT_TOPK) // blk) + N_EXPERTS
    cnt = counts[:, 0].astype(jnp.int32)
    padded = (cnt + blk - 1) // blk * blk
    pend = jnp.cumsum(padded)
    pstart = pend - padded
    dest = pstart[eid[:EXPERT_TOPK]] + rank[:EXPERT_TOPK]
    blk_expert = jnp.minimum(jnp.searchsorted(pend, jnp.arange(n_blk) * blk, side="right"),
                             N_EXPERTS - 1).astype(jnp.int32)
    n_used = (pend[-1:] // blk).astype(jnp.int32)

    xs = _moe_scatter(xn, dest, n_blk * blk)
    ys = _moe_ffn(xs, blk_expert, n_used, w["w_gate"], w["w_up"], w["w_down"])
    return _moe_combine(h, jnp.transpose(wt), dest, ys)


def _prep_weights(g_attn_norm, w_in, w_gla_gate_up, b_gla_gate, g_gla_out, g_q, g_k, w_out, g_ffn_norm,
                  w_group_router, b_group_router, w_expert_router, b_expert_router, w_gate, w_up, w_down):
    d = w_in.shape[0]
    o = np.cumsum((GLA_QK_W, GLA_QK_W, GLA_V_W, GLA_V_W, GLA_GATE_RANK, MOBA_W, MOBA_W, MOBA_W))
    wlr = jnp.pad(w_in[:, o[3]:o[4]], ((0, 0), (0, LANES - GLA_GATE_RANK)))
    wup = jnp.pad(w_gla_gate_up, ((0, LANES - GLA_GATE_RANK), (0, 0)))
    wup_hi = wup.astype(BF16)
    wup_lo = (wup - wup_hi.astype(F32)).astype(BF16)
    wr = jnp.zeros((4 * SUBLANES, d), F32)
    wr = wr.at[0:N_GROUPS].set(w_group_router.T).at[SUBLANES:SUBLANES + N_EXPERTS].set(w_expert_router.T)
    br = jnp.zeros((4 * SUBLANES, 1), F32)
    br = br.at[0:N_GROUPS, 0].set(b_group_router).at[SUBLANES:SUBLANES + N_EXPERTS, 0].set(b_expert_router)
    return {
        "g_attn": g_attn_norm[None, :],
        "wg": w_in[:, :o[3]].astype(BF16),
        "wlr": wlr.astype(BF16),
        "wup": jnp.stack([wup_hi, wup_lo]),
        "b_gate": b_gla_gate[None, :],
        "wm": w_in[:, o[4]:].astype(BF16),
        "g_q": jnp.tile(g_q, MOBA_HEADS)[None, :],
        "g_k": jnp.tile(g_k, MOBA_HEADS)[None, :],
        "g_gla_out": g_gla_out[None, :],
        "wo1": w_out[:GLA_V_W].astype(BF16),
        "wo2": w_out[GLA_V_W:].astype(BF16),
        "g_ffn": g_ffn_norm[None, :],
        "wr": wr.astype(BF16),
        "br": br,
        "w_gate": w_gate.astype(BF16),
        "w_up": w_up.astype(BF16),
        "w_down": w_down.astype(BF16),
    }


def _layer(x, positions, s0, gla_chunk, moba_fn, w):
    bsz, l, d = x.shape
    n = bsz * l
    x2d = x.reshape(n, d)
    cos_tab, sin_tab = _rope_tables(positions)
    gq, gk, gv, gr, la, mq, mk, mv = _inproj(x2d, w, cos_tab, sin_tab)

    lp = -(-l // gla_chunk) * gla_chunk
    seq = lambda a: jnp.pad(a.reshape(bsz, l, -1), ((0, 0), (0, lp - l), (0, 0)))
    go, s_new = _gla(seq(gq), seq(gk), seq(gv), seq(la), s0.reshape(bsz, GLA_QK_W, GLA_DV), gla_chunk)
    go = go[:, :l].reshape(n, GLA_V_W)

    mk3 = mk.reshape(bsz, l, MOBA_W)
    mv3 = mv.reshape(bsz, l, MOBA_W)
    mo = moba_fn(mq.reshape(bsz, l, MOBA_W), mk3, mv3).reshape(n, MOBA_W)

    h, xn, eid, wt, rank, counts = _outproj(x2d, go, gr, mo, w)
    y = _moe(h, xn, eid, wt, rank, counts, w)
    return (y.reshape(bsz, l, d), mk3.reshape(bsz, l, MOBA_HEADS, MOBA_HD), mv3.reshape(bsz, l, MOBA_HEADS, MOBA_HD),
            s_new.reshape(bsz, GLA_HEADS, GLA_DK, GLA_DV))


def kernel(x_prompt, x_sample, cache_k, cache_v, state_gla, page_table, g_attn_norm, w_in, w_gla_gate_up, b_gla_gate,
           g_gla_out, g_q, g_k, w_out, g_ffn_norm, w_group_router, b_group_router, w_expert_router, b_expert_router,
           w_gate, w_up, w_down):
    depth = w_in.shape[0]
    bsz, l, _ = x_prompt.shape
    dbs, t_new, _ = x_sample.shape
    n_phys, page = cache_k.shape[1], cache_k.shape[2]
    past = page_table.shape[1] * page
    assert l % ROW_TILE == 0 and (dbs * t_new) % ROW_TILE == 0 and ROW_TILE % t_new == 0

    pos_p = np.arange(l)
    pos_s = past + (np.arange(ROW_TILE) % t_new)
    chunk_s = max(SUBLANES, 1 << (t_new - 1).bit_length())

    hp, hs = x_prompt, x_sample
    outs = [[] for _ in range(6)]
    for li in range(depth):
        w = _prep_weights(g_attn_norm[li], w_in[li], w_gla_gate_up[li], b_gla_gate[li], g_gla_out[li], g_q[li],
                          g_k[li], w_out[li], g_ffn_norm[li], w_group_router[li], b_group_router[li],
                          w_expert_router[li], b_expert_router[li], w_gate[li], w_up[li], w_down[li])
        s0 = jnp.zeros((bsz, GLA_HEADS, GLA_DK, GLA_DV), F32)
        hp, kp, vp, sp = _layer(hp, pos_p, s0, min(GLA_CHUNK, l), _moba_prompt, w)

        ck = cache_k[li].reshape(n_phys, page, MOBA_W)
        cv = cache_v[li].reshape(n_phys, page, MOBA_W)
        moba_s = lambda q, k, v, ck=ck, cv=cv: _moba_sample(q, k, v, ck, cv, page_table)
        hs, ks, vs, ss = _layer(hs, pos_s, state_gla[li].astype(F32), chunk_s, moba_s, w)
        for lst, val in zip(outs, (kp, vp, sp.astype(state_gla.dtype), ks, vs, ss.astype(state_gla.dtype))):
            lst.append(val)

    return (hp, hs) + tuple(jnp.stack(o) for o in outs)
```

```python
import functools
import math

import jax
import jax.numpy as jnp
import numpy as np
from jax import lax
from jax.experimental import pallas as pl
from jax.experimental.pallas import tpu as pltpu

GLA_HEADS = 4
GLA_DK = 64
GLA_DV = 128
GLA_GATE_RANK = 16
GLA_TAU = 16.0
MOBA_HEADS = 8
MOBA_HD = 64
MOBA_BLOCK = 256
MOBA_TOPK = 3
ROPE_THETA = 10000.0
N_GROUPS = 4
EXPERTS_PER_GROUP = 4
N_EXPERTS = N_GROUPS * EXPERTS_PER_GROUP
EXPERT_TOPK = 2
EPS = 1e-6

GLA_QK_W = GLA_HEADS * GLA_DK
GLA_V_W = GLA_HEADS * GLA_DV
MOBA_W = MOBA_HEADS * MOBA_HD

LANES = 128
SUBLANES = 8
VMEM_LIMIT_BYTES = 56 * 1024 * 1024

ROW_TILE = 256
GLA_CHUNK = 256
MOE_BLOCK = 256
SAMPLE_PAGES_PER_STEP = 8
MOBA_GROUP = 4
F32 = jnp.float32
BF16 = jnp.bfloat16
NEG = -1e30
LOG2E = math.log2(math.e)
V_AUG_ROWS = MOBA_HD + 16


def _dot(a, b):
    return jnp.dot(a, b, preferred_element_type=F32)


def _dot_nt(a, b):
    return lax.dot_general(a, b, (((1,), (1,)), ((), ())), preferred_element_type=F32)


def _dot_tn(a, b):
    return lax.dot_general(a, b, (((0,), (0,)), ((), ())), preferred_element_type=F32)


def _split2(x):
    hi = x.astype(BF16)
    lo = (x - hi.astype(F32)).astype(BF16)
    return hi, lo


def _params(n_axes):
    return pltpu.CompilerParams(dimension_semantics=("arbitrary",) * n_axes,
                                vmem_limit_bytes=VMEM_LIMIT_BYTES)


def _head_norm(y, g):
    lo = lax.broadcasted_iota(jnp.int32, (1, LANES), 1) < MOBA_HD
    outs = []
    for p in range(MOBA_W // LANES):
        yp = y[:, p * LANES:(p + 1) * LANES]
        sq = yp * yp
        s_lo = jnp.sum(jnp.where(lo, sq, 0.0), axis=-1, keepdims=True)
        s_hi = jnp.sum(jnp.where(lo, 0.0, sq), axis=-1, keepdims=True)
        ms = jnp.where(lo, s_lo, s_hi) * (1.0 / MOBA_HD)
        outs.append(yp * lax.rsqrt(ms + EPS))
    return jnp.concatenate(outs, axis=1) * g


def _rope(y, cos, sin_signed):
    n = y.shape[1]
    first = (lax.broadcasted_iota(jnp.int32, (1, n), 1) % MOBA_HD) < (MOBA_HD // 2)
    rot = jnp.where(first, pltpu.roll(y, n - MOBA_HD // 2, 1), pltpu.roll(y, MOBA_HD // 2, 1))
    return y * cos + rot * sin_signed


def _inproj_kernel(x_ref, gin_ref, wg_ref, wlr_ref, wup_ref, bgate_ref, wm_ref, gq_ref, gk_ref, cos_ref, sin_ref,
                   oq_ref, ok_ref, ov_ref, or_ref, ola_ref, omq_ref, omk_ref, omv_ref):
    x = x_ref[...]
    ms = jnp.mean(x * x, axis=-1, keepdims=True)
    xn = (x * lax.rsqrt(ms + EPS) * gin_ref[...]).astype(BF16)

    yg = _dot(xn, wg_ref[...])
    oq_ref[...] = yg[:, :GLA_QK_W] * (GLA_DK ** -0.5)
    ok_ref[...] = yg[:, GLA_QK_W:2 * GLA_QK_W]
    ov_ref[...] = yg[:, 2 * GLA_QK_W:2 * GLA_QK_W + GLA_V_W]
    or_ref[...] = yg[:, 2 * GLA_QK_W + GLA_V_W:]

    lr = _dot(xn, wlr_ref[...])
    lr_hi, lr_lo = _split2(lr)
    wu_hi = wup_ref[0]
    wu_lo = wup_ref[1]
    z = _dot(lr_hi, wu_hi) + _dot(lr_lo, wu_hi) + _dot(lr_hi, wu_lo) + bgate_ref[...]
    ola_ref[...] = (jnp.minimum(z, 0.0) - jnp.log(1.0 + jnp.exp(-jnp.abs(z)))) * (1.0 / GLA_TAU)

    ym = _dot(xn, wm_ref[...])
    reps = MOBA_W // LANES
    cos = jnp.concatenate([cos_ref[...]] * reps, axis=1)
    sin = jnp.concatenate([sin_ref[...]] * reps, axis=1)
    omq_ref[...] = _rope(_head_norm(ym[:, :MOBA_W], gq_ref[...]), cos, sin)
    omk_ref[...] = _rope(_head_norm(ym[:, MOBA_W:2 * MOBA_W], gk_ref[...]), cos, sin)
    omv_ref[...] = ym[:, 2 * MOBA_W:]


def _inproj(x2d, w, cos_tab, sin_tab):
    n, d = x2d.shape
    t = ROW_TILE
    assert n % t == 0 and cos_tab.shape[0] % t == 0
    n_pos_tiles = cos_tab.shape[0] // t
    row = lambda i: (i, 0)
    fixed = lambda i: (0, 0)
    widths = (GLA_QK_W, GLA_QK_W, GLA_V_W, GLA_V_W, GLA_QK_W, MOBA_W, MOBA_W, MOBA_W)
    return pl.pallas_call(
        _inproj_kernel,
        grid=(n // t,),
        in_specs=[
            pl.BlockSpec((t, d), row),
            pl.BlockSpec((1, d), fixed),
            pl.BlockSpec(w["wg"].shape, fixed),
            pl.BlockSpec(w["wlr"].shape, fixed),
            pl.BlockSpec(w["wup"].shape, lambda i: (0, 0, 0)),
            pl.BlockSpec((1, GLA_QK_W), fixed),
            pl.BlockSpec(w["wm"].shape, fixed),
            pl.BlockSpec((1, MOBA_W), fixed),
            pl.BlockSpec((1, MOBA_W), fixed),
            pl.BlockSpec((t, LANES), lambda i: (i % n_pos_tiles, 0)),
            pl.BlockSpec((t, LANES), lambda i: (i % n_pos_tiles, 0)),
        ],
        out_specs=[pl.BlockSpec((t, wd), row) for wd in widths],
        out_shape=[jax.ShapeDtypeStruct((n, wd), F32) for wd in widths],
        compiler_params=_params(1),
        name="inproj",
    )(x2d, w["g_attn"], w["wg"], w["wlr"], w["wup"], w["b_gate"], w["wm"], w["g_q"], w["g_k"], cos_tab, sin_tab)


def _rope_tables(positions):
    half = MOBA_HD // 2
    inv = ROPE_THETA ** (-np.arange(half, dtype=np.float64) / half)
    ang = positions.astype(np.float64)[:, None] * inv[None, :]
    cos = np.cos(ang)
    sin = np.sin(ang)
    cos_h = np.concatenate([cos, cos], axis=1)
    sin_h = np.concatenate([-sin, sin], axis=1)
    reps = LANES // MOBA_HD
    return (jnp.asarray(np.tile(cos_h, (1, reps)), F32), jnp.asarray(np.tile(sin_h, (1, reps)), F32))


def _gla_level_matrix(c):
    t = np.arange(c)[:, None]
    j = np.arange(c)[None, :]
    mats = [(j <= t)]
    m = c // 2
    while m >= 1:
        base = (t // (2 * m)) * (2 * m)
        ref = base + m - 1
        upper = (t - base) >= m
        mats.append(np.where(upper, (j > ref) & (j <= t), (j > t) & (j <= ref)))
        m //= 2
    return jnp.asarray(np.concatenate(mats, axis=0).astype(np.float32), BF16)


def _gla_kernel(q_ref, k_ref, v_ref, la_ref, s0_ref, lvl_ref, o_ref, sout_ref, s_ref, *, c):
    ci = pl.program_id(1)
    n_lev = int(math.log2(c))
    pair_w = 2 * GLA_DK

    @pl.when(ci == 0)
    def _():
        s_ref[...] = s0_ref[...]

    la = la_ref[...]
    la_hi, la_lo = _split2(la)
    lvl = lvl_ref[...]
    x_all = _dot(lvl, la_hi) + _dot(lvl, la_lo)
    b = x_all[0:c]
    b_last = b[c - 1:c, :]
    q = q_ref[...]
    k = k_ref[...]
    qe = q * jnp.exp(b)
    kd = (k * jnp.exp(b_last - b)).astype(BF16)

    ti = lax.broadcasted_iota(jnp.int32, (c, c), 0)
    si = lax.broadcasted_iota(jnp.int32, (c, c), 1)
    trow = lax.broadcasted_iota(jnp.int32, (c, 1), 0)
    lane = lax.broadcasted_iota(jnp.int32, (1, pair_w), 1)
    ones_cw = jnp.ones((c, pair_w), BF16)
    srow = lax.broadcasted_iota(jnp.int32, (pair_w, 1), 0)

    for p in range(GLA_HEADS // 2):
        cols = slice(p * pair_w, (p + 1) * pair_w)
        q_p = q[:, cols]
        k_p = k[:, cols]
        qe_p = qe[:, cols]
        head_lanes = (lane < GLA_DK, lane >= GLA_DK)
        s_pair = s_ref[cols, :]
        s_pair_b = s_pair.astype(BF16)

        a = [jnp.where(ti == si, _dot_nt(jnp.where(hm, q_p, 0.0).astype(BF16), k_p.astype(BF16)), 0.0)
             for hm in head_lanes]
        for lev in range(n_lev):
            m_log = n_lev - 1 - lev
            e = jnp.exp(x_all[(lev + 1) * c:(lev + 2) * c, cols])
            upper = ((trow >> m_log) & 1) == 1
            kt = jnp.where(upper, 0.0, k_p * e).astype(BF16)
            qt = jnp.where(upper, q_p * e, 0.0)
            same = (ti >> (m_log + 1)) == (si >> (m_log + 1))
            for hh in range(2):
                al = _dot_nt(jnp.where(head_lanes[hh], qt, 0.0).astype(BF16), kt)
                a[hh] = a[hh] + jnp.where(same, al, 0.0)

        u = []
        for hh in range(2):
            h = 2 * p + hh
            v_h = v_ref[:, h * GLA_DV:(h + 1) * GLA_DV].astype(BF16)
            o_h = _dot(a[hh].astype(BF16), v_h) + _dot(jnp.where(head_lanes[hh], qe_p, 0.0).astype(BF16), s_pair_b)
            o_ref[:, h * GLA_DV:(h + 1) * GLA_DV] = o_h
            u.append(_dot_tn(kd[:, cols], v_h))
        bl_rows = _dot_tn(la_hi[:, cols], ones_cw) + _dot_tn(la_lo[:, cols], ones_cw)
        s_ref[cols, :] = jnp.exp(bl_rows[:, :GLA_DV]) * s_pair + jnp.where(srow < GLA_DK, u[0], u[1])

    @pl.when(ci == pl.num_programs(1) - 1)
    def _():
        sout_ref[...] = s_ref[...]


def _gla(q, k, v, la, s0, c):
    bsz, l, _ = q.shape
    assert l % c == 0 and (c & (c - 1)) == 0 and GLA_DV == 2 * GLA_DK
    lvl = _gla_level_matrix(c)
    tok = lambda b, i: (b, i, 0)
    per_b = lambda b, i: (b, 0, 0)
    return pl.pallas_call(
        functools.partial(_gla_kernel, c=c),
        grid=(bsz, l // c),
        in_specs=[
            pl.BlockSpec((None, c, GLA_QK_W), tok),
            pl.BlockSpec((None, c, GLA_QK_W), tok),
            pl.BlockSpec((None, c, GLA_V_W), tok),
            pl.BlockSpec((None, c, GLA_QK_W), tok),
            pl.BlockSpec((None, GLA_QK_W, GLA_DV), per_b),
            pl.BlockSpec(lvl.shape, lambda b, i: (0, 0)),
        ],
        out_specs=[pl.BlockSpec((None, c, GLA_V_W), tok), pl.BlockSpec((None, GLA_QK_W, GLA_DV), per_b)],
        out_shape=[jax.ShapeDtypeStruct((bsz, l, GLA_V_W), F32), jax.ShapeDtypeStruct((bsz, GLA_QK_W, GLA_DV), F32)],
        scratch_shapes=[pltpu.VMEM((GLA_QK_W, GLA_DV), F32)],
        compiler_params=_params(2),
        name="gla",
    )(q, k, v, la, s0, lvl)


def _select_topk_blocks(gate, n_valid):
    row = lax.broadcasted_iota(jnp.int32, gate.shape, 0)
    g = jnp.where(row < n_valid, gate, NEG)
    sel = jnp.zeros(gate.shape, F32)
    for r in range(MOBA_TOPK):
        mx = jnp.max(g, axis=0, keepdims=True)
        idx = jnp.min(jnp.where(g == mx, row, gate.shape[0]), axis=0, keepdims=True)
        pick = row == jnp.where(r < n_valid, idx, -1)
        sel = jnp.where(pick, 1.0, sel)
        g = jnp.where(pick, NEG, g)
    return sel


def _moba_prompt_kernel(q_ref, k_ref, v_ref, o_ref, kmean_ref, sel_ref, *, nb):
    h = pl.program_id(1)
    qi = pl.program_id(2)
    blk = MOBA_BLOCK

    @pl.when(qi == 0)
    def _():
        kmean_ref[...] = jnp.zeros(kmean_ref.shape, F32)

        def body(n, carry):
            kb = k_ref[n].astype(F32)
            kmean_ref[pl.ds(n, 1), :] = jnp.sum(kb, axis=0, keepdims=True) * (1.0 / blk)
            return carry

        lax.fori_loop(0, nb, body, 0)

    q = q_ref[...]
    row = lax.broadcasted_iota(jnp.int32, (2 * MOBA_HD, blk), 0)
    mine = (row // MOBA_HD) == (h % 2)
    q_pair = jnp.where(mine, jnp.concatenate([q, q], axis=0), jnp.zeros((), BF16))

    km_hi, km_lo = _split2(kmean_ref[...])
    gate = _dot(km_hi, q_pair) + _dot(km_lo, q_pair)
    sel_ref[...] = _select_topk_blocks(gate, qi)

    def update(carry, scores, v_tiles, picked):
        m, acc = carry
        m_new = m
        for s, pk in zip(scores, picked):
            bm = jnp.max(s, axis=0, keepdims=True)
            m_new = jnp.maximum(m_new, bm if pk is None else jnp.where(pk, bm, NEG))
        acc = acc * jnp.exp2(m - m_new)
        for s, v_t, pk in zip(scores, v_tiles, picked):
            c = _dot(v_t, jnp.exp2(s - m_new).astype(BF16))
            acc = acc + (c if pk is None else jnp.where(pk, c, 0.0))
        return m_new, acc

    def past_group(gi, carry):
        scores, v_tiles, picked = [], [], []
        for g in range(MOBA_GROUP):
            n = jnp.minimum(gi * MOBA_GROUP + g, nb - 1)
            scores.append(_dot(k_ref[n], q_pair))
            picked.append(sel_ref[pl.ds(n, 1), :] > 0.5)
            v_tiles.append(v_ref[n])
        return update(carry, scores, v_tiles, picked)

    carry = (jnp.full((1, blk), NEG, F32), jnp.zeros((v_ref.shape[1], blk), F32))
    carry = lax.fori_loop(0, (qi + MOBA_GROUP - 1) // MOBA_GROUP, past_group, carry)

    s = _dot(k_ref[qi], q_pair)
    causal = lax.broadcasted_iota(jnp.int32, (blk, blk), 0) <= lax.broadcasted_iota(jnp.int32, (blk, blk), 1)
    _, acc = update(carry, [jnp.where(causal, s, NEG)], [v_ref[qi]], [None])
    o_ref[...] = jnp.transpose(acc[:MOBA_HD] / acc[MOBA_HD:MOBA_HD + 1])


def _moba_prompt(mq, mk, mv):
    bsz, l, _ = mq.shape
    blk = MOBA_BLOCK
    assert l % blk == 0
    nb = l // blk
    nb_pad = -(-nb // SUBLANES) * SUBLANES
    hp = MOBA_HEADS // 2
    qt = (mq * (MOBA_HD ** -0.5 * LOG2E)).astype(BF16).reshape(bsz, nb, blk, MOBA_HEADS, MOBA_HD)
    qt = qt.transpose(0, 3, 1, 4, 2)
    vt = mv.astype(BF16).reshape(bsz, nb, blk, MOBA_HEADS, MOBA_HD).transpose(0, 3, 1, 4, 2)
    ones_rows = jnp.zeros((bsz, MOBA_HEADS, nb, V_AUG_ROWS - MOBA_HD, blk), BF16).at[:, :, :, 0, :].set(1.0)
    vt = jnp.concatenate([vt, ones_rows], axis=3)
    kp = mk.astype(BF16).reshape(bsz, nb, blk, hp, 2 * MOBA_HD).transpose(0, 3, 1, 2, 4)
    out = pl.pallas_call(
        functools.partial(_moba_prompt_kernel, nb=nb),
        grid=(bsz, MOBA_HEADS, nb),
        in_specs=[
            pl.BlockSpec((None, None, None, MOBA_HD, blk), lambda b, h, i: (b, h, i, 0, 0)),
            pl.BlockSpec((None, None, nb, blk, 2 * MOBA_HD), lambda b, h, i: (b, h // 2, 0, 0, 0)),
            pl.BlockSpec((None, None, nb, V_AUG_ROWS, blk), lambda b, h, i: (b, h, 0, 0, 0)),
        ],
        out_specs=pl.BlockSpec((None, None, blk, MOBA_HD), lambda b, h, i: (b, h, i, 0)),
        out_shape=jax.ShapeDtypeStruct((bsz, MOBA_HEADS, l, MOBA_HD), F32),
        scratch_shapes=[pltpu.VMEM((nb_pad, 2 * MOBA_HD), F32), pltpu.VMEM((nb_pad, blk), F32)],
        compiler_params=_params(3),
        name="moba_prompt",
    )(qt, kp, vt)
    return out.transpose(0, 2, 1, 3).reshape(bsz, l, MOBA_W)


def _select_topk_lanes(gate, n_valid):
    lane = lax.broadcasted_iota(jnp.int32, gate.shape, 1)
    g = jnp.where(lane < n_valid, gate, NEG)
    sel = jnp.zeros(gate.shape, F32)
    for r in range(min(MOBA_TOPK, n_valid)):
        mx = jnp.max(g, axis=1, keepdims=True)
        idx = jnp.min(jnp.where(g == mx, lane, gate.shape[1]), axis=1, keepdims=True)
        pick = lane == idx
        sel = jnp.where(pick, 1.0, sel)
        g = jnp.where(pick, NEG, g)
    return sel


def _moba_sample_kernel(pt_ref, qb_ref, kn_ref, vn_ref, *refs, npp, n_steps, n_pages, page, t_new):
    k_refs = refs[:npp]
    v_refs = refs[npp:2 * npp]
    o_ref = refs[2 * npp]
    s_ref, p_ref, acc_ref = refs[2 * npp + 1:]
    j = pl.program_id(1)
    blk = MOBA_BLOCK
    n_blk = (n_pages * page) // blk
    past = n_pages * page
    n_c = qb_ref.shape[0]
    t_pad = n_c // MOBA_HEADS

    def flat(ref):
        return ref[...].reshape(MOBA_W, page)

    @pl.when(j < n_steps)
    def _():
        qb = qb_ref[...]
        for i in range(0, npp, 2):
            w = jnp.concatenate([flat(k_refs[i]), flat(k_refs[i + 1])], axis=1).astype(BF16)
            cols = pl.ds(pl.multiple_of((j * npp + i) * page, 2 * page), 2 * page)
            s_ref[:, cols] = _dot(qb, w)

    @pl.when(j == n_steps - 1)
    def _():
        qb = qb_ref[...]
        lane = lax.broadcasted_iota(jnp.int32, (n_c, LANES), 1)

        def blk_cols(n):
            return pl.ds(pl.multiple_of(n * blk, blk), blk)

        def gate_body(n, gate):
            gsum = jnp.sum(s_ref[:, blk_cols(n)], axis=1, keepdims=True) * (1.0 / blk)
            return jnp.where(lane == n, gsum, gate)

        unroll = math.gcd(n_blk, 8)
        gate = lax.fori_loop(0, n_blk, gate_body, jnp.zeros((n_c, LANES), F32), unroll=unroll)
        sel = _select_topk_lanes(gate, n_blk)

        s_own = _dot(qb, kn_ref[...].astype(BF16))
        key_t = lax.broadcasted_iota(jnp.int32, (n_c, page), 1)
        row_t = lax.broadcasted_iota(jnp.int32, (n_c, page), 0) % t_pad
        s_own = jnp.where((key_t < t_new) & (key_t <= row_t), s_own, NEG)

        def masked(n):
            picked = jnp.sum(jnp.where(lane == n, sel, 0.0), axis=1, keepdims=True) > 0.5
            return jnp.where(picked, s_ref[:, blk_cols(n)], NEG)

        def max_body(n, m):
            return jnp.maximum(m, masked(n))

        m_run = lax.fori_loop(0, n_blk, max_body, jnp.full((n_c, blk), NEG, F32), unroll=unroll)
        m = jnp.maximum(jnp.max(m_run, axis=1, keepdims=True), jnp.max(s_own, axis=1, keepdims=True))

        def exp_body(n, l_run):
            e = jnp.exp(masked(n) - m)
            s_ref[:, blk_cols(n)] = e
            return l_run + e

        l_run = lax.fori_loop(0, n_blk, exp_body, jnp.zeros((n_c, blk), F32), unroll=unroll)
        e_own = jnp.exp(s_own - m)
        inv = 1.0 / (jnp.sum(l_run, axis=1, keepdims=True) + jnp.sum(e_own, axis=1, keepdims=True))

        def norm_body(n, carry):
            p_ref[:, blk_cols(n)] = (s_ref[:, blk_cols(n)] * inv).astype(BF16)
            return carry

        lax.fori_loop(0, n_blk, norm_body, 0, unroll=unroll)
        p_ref[:, past:past + page] = (e_own * inv).astype(BF16)
        acc_ref[...] = jnp.zeros(acc_ref.shape, F32)

    @pl.when(j >= n_steps)
    def _():
        v_cat = jnp.concatenate([flat(v_refs[i]) for i in range(npp)], axis=1).astype(BF16)
        cols = pl.ds(pl.multiple_of((j - n_steps) * (npp * page), npp * page), npp * page)
        acc_ref[...] += _dot_nt(v_cat, p_ref[:, cols])

    @pl.when(j == 2 * n_steps - 1)
    def _():
        o_ref[...] = acc_ref[...] + _dot_nt(vn_ref[...].astype(BF16), p_ref[:, past:past + page])


def _moba_sample(mq, mk, mv, cache_kt, cache_vt, layer, page_table):
    dbs, t_new, _ = mq.shape
    page = cache_kt.shape[-1]
    n_pages = page_table.shape[1]
    npp = SAMPLE_PAGES_PER_STEP
    t_pad = t_new + t_new % 2
    n_c = MOBA_HEADS * t_pad
    assert n_pages % npp == 0 and npp % 2 == 0 and page == LANES
    assert (n_pages * page) % MOBA_BLOCK == 0 and (n_pages * page) // MOBA_BLOCK <= LANES and t_new <= page
    n_steps = n_pages // npp

    q4 = (mq * (MOBA_HD ** -0.5)).reshape(dbs, t_new, MOBA_HEADS, MOBA_HD)
    q4 = jnp.pad(q4, ((0, 0), (0, t_pad - t_new), (0, 0), (0, 0)))
    eye = jnp.eye(MOBA_HEADS, dtype=F32)
    qb = (q4[:, :, :, None, :] * eye[None, None, :, :, None]).transpose(0, 2, 1, 3, 4)
    qb = qb.reshape(dbs, n_c, MOBA_W).astype(BF16)
    kn = jnp.pad(jnp.transpose(mk, (0, 2, 1)), ((0, 0), (0, 0), (0, page - t_new)))
    vn = jnp.pad(jnp.transpose(mv, (0, 2, 1)), ((0, 0), (0, 0), (0, page - t_new)))
    pt_flat = page_table.reshape(-1).astype(jnp.int32)

    def k_map(i):
        return lambda b, j, pt: (layer, pt[b * n_pages + jnp.minimum(j, n_steps - 1) * npp + i], 0, 0, 0)

    def v_map(i):
        return lambda b, j, pt: (layer, pt[b * n_pages + jnp.maximum(j - n_steps, 0) * npp + i], 0, 0, 0)

    per_b = lambda b, j, pt: (b, 0, 0)
    page_block = (None, None, MOBA_HEADS, MOBA_HD, page)
    grid_spec = pltpu.PrefetchScalarGridSpec(
        num_scalar_prefetch=1,
        grid=(dbs, 2 * n_steps),
        in_specs=[pl.BlockSpec((None, n_c, MOBA_W), per_b),
                  pl.BlockSpec((None, MOBA_W, page), per_b),
                  pl.BlockSpec((None, MOBA_W, page), per_b)]
                 + [pl.BlockSpec(page_block, k_map(i)) for i in range(npp)]
                 + [pl.BlockSpec(page_block, v_map(i)) for i in range(npp)],
        out_specs=pl.BlockSpec((None, MOBA_W, n_c), per_b),
        scratch_shapes=[
            pltpu.VMEM((n_c, n_pages * page), F32),
            pltpu.VMEM((n_c, (n_pages + 1) * page), BF16),
            pltpu.VMEM((MOBA_W, n_c), F32),
        ],
    )
    out_t = pl.pallas_call(
        functools.partial(_moba_sample_kernel, npp=npp, n_steps=n_steps, n_pages=n_pages, page=page, t_new=t_new),
        grid_spec=grid_spec,
        out_shape=jax.ShapeDtypeStruct((dbs, MOBA_W, n_c), F32),
        compiler_params=_params(2),
        name="moba_sample",
    )(pt_flat, qb, kn, vn, *([cache_kt] * npp), *([cache_vt] * npp))
    o5 = out_t.reshape(dbs, MOBA_HEADS, MOBA_HD, MOBA_HEADS, t_pad)
    o = jnp.sum(o5 * eye[None, :, None, :, None], axis=3)
    return o.transpose(0, 3, 1, 2)[:, :t_new].reshape(dbs, t_new, MOBA_W)


def _outproj_kernel(x_ref, go_ref, gr_ref, mo_ref, ggo_ref, wo1_ref, wo2_ref, gffn_ref, wr_ref, br_ref,
                    h_ref, xn_ref, eid_ref, wt_ref, rank_ref, cnt_ref, carry_ref):
    i = pl.program_id(0)
    t = x_ref.shape[0]

    @pl.when(i == 0)
    def _():
        carry_ref[...] = jnp.zeros(carry_ref.shape, F32)

    gr = gr_ref[...]
    gate = gr / (1.0 + jnp.exp(-gr))
    parts = []
    for hd in range(GLA_HEADS):
        cols = slice(hd * GLA_DV, (hd + 1) * GLA_DV)
        o = go_ref[:, cols]
        ms = jnp.mean(o * o, axis=-1, keepdims=True)
        parts.append(o * lax.rsqrt(ms + EPS) * ggo_ref[...] * gate[:, cols])
    a1 = jnp.concatenate(parts, axis=1).astype(BF16)
    h = x_ref[...] + _dot(a1, wo1_ref[...]) + _dot(mo_ref[...].astype(BF16), wo2_ref[...])
    h_ref[...] = h

    ms = jnp.mean(h * h, axis=-1, keepdims=True)
    xn = h * lax.rsqrt(ms + EPS) * gffn_ref[...]
    xn_ref[...] = xn
    logits = _dot_nt(wr_ref[...], xn.astype(BF16)) + br_ref[...]

    gl = logits[0:N_GROUPS]
    grow = lax.broadcasted_iota(jnp.int32, gl.shape, 0)
    gmax = jnp.max(gl, axis=0, keepdims=True)
    g_sel = jnp.min(jnp.where(gl == gmax, grow, N_GROUPS), axis=0, keepdims=True)
    gw = 1.0 / jnp.sum(jnp.exp(gl - gmax), axis=0, keepdims=True)

    el = logits[SUBLANES:SUBLANES + N_EXPERTS]
    erow = lax.broadcasted_iota(jnp.int32, el.shape, 0)
    in_grp = (erow // EXPERTS_PER_GROUP) == g_sel
    emax = jnp.max(jnp.where(in_grp, el, NEG), axis=0, keepdims=True)
    ee = jnp.where(in_grp, jnp.exp(el - emax), 0.0)
    ep = ee / jnp.sum(ee, axis=0, keepdims=True)
    cand = jnp.where(in_grp, ep, -1.0)
    p1 = jnp.max(cand, axis=0, keepdims=True)
    i1 = jnp.min(jnp.where(cand == p1, erow, N_EXPERTS), axis=0, keepdims=True)
    cand = jnp.where(erow == i1, -1.0, cand)
    p2 = jnp.max(cand, axis=0, keepdims=True)
    i2 = jnp.min(jnp.where(cand == p2, erow, N_EXPERTS), axis=0, keepdims=True)
    denom = p1 + p2
    zero_i = jnp.zeros((SUBLANES - EXPERT_TOPK, t), jnp.int32)
    zero_f = jnp.zeros((SUBLANES - EXPERT_TOPK, t), F32)
    eid_ref[...] = jnp.concatenate([i1, i2, zero_i], axis=0)
    wt_ref[...] = jnp.concatenate([p1 / denom * gw, p2 / denom * gw, zero_f], axis=0)

    oh1 = erow == i1
    oh2 = erow == i2
    hits = jnp.where(oh1 | oh2, 1.0, 0.0)
    before = (lax.broadcasted_iota(jnp.int32, (t, t), 0) < lax.broadcasted_iota(jnp.int32, (t, t), 1))
    prefix = _dot(hits.astype(BF16), jnp.where(before, 1.0, 0.0).astype(BF16)) + carry_ref[...]
    r1 = jnp.sum(jnp.where(oh1, prefix, 0.0), axis=0, keepdims=True)
    r2 = jnp.sum(jnp.where(oh2, prefix, 0.0), axis=0, keepdims=True)
    rank_ref[...] = jnp.concatenate([r1.astype(jnp.int32), r2.astype(jnp.int32), zero_i], axis=0)
    carry_ref[...] += jnp.sum(hits, axis=1, keepdims=True)
    cnt_ref[...] = jnp.broadcast_to(carry_ref[...], cnt_ref.shape)


def _outproj(x2d, go, gr, mo, w):
    n, d = x2d.shape
    t = ROW_TILE
    assert n % t == 0
    row = lambda i: (i, 0)
    col = lambda i: (0, i)
    fixed = lambda i: (0, 0)
    return pl.pallas_call(
        _outproj_kernel,
        grid=(n // t,),
        in_specs=[
            pl.BlockSpec((t, d), row),
            pl.BlockSpec((t, GLA_V_W), row),
            pl.BlockSpec((t, GLA_V_W), row),
            pl.BlockSpec((t, MOBA_W), row),
            pl.BlockSpec((1, GLA_DV), fixed),
            pl.BlockSpec(w["wo1"].shape, fixed),
            pl.BlockSpec(w["wo2"].shape, fixed),
            pl.BlockSpec((1, d), fixed),
            pl.BlockSpec(w["wr"].shape, fixed),
            pl.BlockSpec(w["br"].shape, fixed),
        ],
        out_specs=[pl.BlockSpec((t, d), row), pl.BlockSpec((t, d), row),
                   pl.BlockSpec((SUBLANES, t), col), pl.BlockSpec((SUBLANES, t), col),
                   pl.BlockSpec((SUBLANES, t), col), pl.BlockSpec((N_EXPERTS, LANES), fixed)],
        out_shape=[jax.ShapeDtypeStruct((n, d), F32), jax.ShapeDtypeStruct((n, d), F32),
                   jax.ShapeDtypeStruct((SUBLANES, n), jnp.int32), jax.ShapeDtypeStruct((SUBLANES, n), F32),
                   jax.ShapeDtypeStruct((SUBLANES, n), jnp.int32), jax.ShapeDtypeStruct((N_EXPERTS, LANES), F32)],
        scratch_shapes=[pltpu.VMEM((N_EXPERTS, 1), F32)],
        compiler_params=_params(1),
        name="outproj",
    )(x2d, go, gr, mo, w["g_gla_out"], w["wo1"], w["wo2"], w["g_ffn"], w["wr"], w["br"])


def _row_copy(src_ref, src_row, dst_ref, dst_row, sem):
    return pltpu.make_async_copy(src_ref.at[pl.ds(src_row, 1)], dst_ref.at[pl.ds(dst_row, 1)], sem)


def _moe_scatter_kernel(dest_ref, x_ref, xs_in_ref, xs_ref, sem):
    del xs_in_ref
    t = x_ref.shape[0]

    def start(r, carry):
        for kk in range(EXPERT_TOPK):
            _row_copy(x_ref, r, xs_ref, dest_ref[kk, r], sem).start()
        return carry

    def wait(r, carry):
        for kk in range(EXPERT_TOPK):
            _row_copy(x_ref, 0, xs_ref, 0, sem).wait()
        return carry

    lax.fori_loop(0, t, start, 0)
    lax.fori_loop(0, t, wait, 0)


def _moe_scatter(xn, dest, n_rows):
    n, d = xn.shape
    t = ROW_TILE
    xs0 = jnp.zeros((n_rows, d), F32)
    return pl.pallas_call(
        _moe_scatter_kernel,
        grid=(n // t,),
        in_specs=[
            pl.BlockSpec((EXPERT_TOPK, t), lambda i: (0, i), memory_space=pltpu.SMEM),
            pl.BlockSpec((t, d), lambda i: (i, 0)),
            pl.BlockSpec(memory_space=pl.ANY),
        ],
        out_specs=pl.BlockSpec(memory_space=pl.ANY),
        out_shape=jax.ShapeDtypeStruct((n_rows, d), F32),
        scratch_shapes=[pltpu.SemaphoreType.DMA(())],
        input_output_aliases={2: 0},
        compiler_params=_params(1),
        name="moe_scatter",
    )(dest, xn, xs0)


def _moe_ffn_kernel(be_ref, nu_ref, xs_ref, wg_ref, wu_ref, wd_ref, ys_ref):
    i = pl.program_id(0)

    @pl.when(i < nu_ref[0])
    def _():
        x = xs_ref[...].astype(BF16)
        g = _dot(x, wg_ref[...])
        u = _dot(x, wu_ref[...])
        hmid = (g / (1.0 + jnp.exp(-g)) * u).astype(BF16)
        ys_ref[...] = _dot(hmid, wd_ref[...])

    @pl.when(i >= nu_ref[0])
    def _():
        ys_ref[...] = jnp.zeros(ys_ref.shape, F32)


def _moe_ffn(xs, blk_expert, n_used, wg, wu, wd):
    r, d = xs.shape
    de = wg.shape[2]
    n_blk = r // MOE_BLOCK
    rows = lambda i, be, nu: (jnp.minimum(i, nu[0] - 1), 0)
    wsel = lambda i, be, nu: (be[jnp.minimum(i, nu[0] - 1)], 0, 0)
    grid_spec = pltpu.PrefetchScalarGridSpec(
        num_scalar_prefetch=2,
        grid=(n_blk,),
        in_specs=[
            pl.BlockSpec((MOE_BLOCK, d), rows),
            pl.BlockSpec((None, d, de), wsel),
            pl.BlockSpec((None, d, de), wsel),
            pl.BlockSpec((None, de, d), wsel),
        ],
        out_specs=pl.BlockSpec((MOE_BLOCK, d), lambda i, be, nu: (i, 0)),
    )
    return pl.pallas_call(
        _moe_ffn_kernel,
        grid_spec=grid_spec,
        out_shape=jax.ShapeDtypeStruct((r, d), F32),
        compiler_params=_params(1),
        name="moe_ffn",
    )(blk_expert, n_used, xs, wg, wu, wd)


def _moe_combine_kernel(dest_ref, h_ref, wt_ref, ys_ref, y_ref, buf_ref, sem):
    t = h_ref.shape[0]

    def start(r, carry):
        for kk in range(EXPERT_TOPK):
            _row_copy(ys_ref, dest_ref[kk, r], buf_ref.at[kk], r, sem).start()
        return carry

    def wait(r, carry):
        for kk in range(EXPERT_TOPK):
            _row_copy(ys_ref, 0, buf_ref.at[kk], 0, sem).wait()
        return carry

    lax.fori_loop(0, t, start, 0)
    lax.fori_loop(0, t, wait, 0)
    y = h_ref[...]
    for kk in range(EXPERT_TOPK):
        y = y + buf_ref[kk] * wt_ref[:, kk:kk + 1]
    y_ref[...] = y


def _moe_combine(h, wt_rows, dest, ys):
    n, d = h.shape
    t = ROW_TILE
    return pl.pallas_call(
        _moe_combine_kernel,
        grid=(n // t,),
        in_specs=[
            pl.BlockSpec((EXPERT_TOPK, t), lambda i: (0, i), memory_space=pltpu.SMEM),
            pl.BlockSpec((t, d), lambda i: (i, 0)),
            pl.BlockSpec((t, SUBLANES), lambda i: (i, 0)),
            pl.BlockSpec(memory_space=pl.ANY),
        ],
        out_specs=pl.BlockSpec((t, d), lambda i: (i, 0)),
        out_shape=jax.ShapeDtypeStruct((n, d), F32),
        scratch_shapes=[pltpu.VMEM((EXPERT_TOPK, t, d), F32), pltpu.SemaphoreType.DMA(())],
        compiler_params=_params(1),
        name="moe_combine",
    )(dest, h, wt_rows, ys)


def _moe(h, xn, eid, wt, rank, counts, w):
    n, d = h.shape
    blk = MOE_BLOCK
    n_blk = -(-(n * EXPERT_TOPK) // blk) + N_EXPERTS
    cnt = counts[:, 0].astype(jnp.int32)
    padded = (cnt + blk - 1) // blk * blk
    pend = jnp.cumsum(padded)
    pstart = pend - padded
    dest = rank[:EXPERT_TOPK]
    for e in range(N_EXPERTS):
        dest = dest + jnp.where(eid[:EXPERT_TOPK] == e, pstart[e], 0)
    blk_expert = jnp.minimum(jnp.searchsorted(pend, jnp.arange(n_blk) * blk, side="right"),
                             N_EXPERTS - 1).astype(jnp.int32)
    n_used = (pend[-1:] // blk).astype(jnp.int32)

    xs = _moe_scatter(xn, dest, n_blk * blk)
    ys = _moe_ffn(xs, blk_expert, n_used, w["w_gate"], w["w_up"], w["w_down"])
    return _moe_combine(h, jnp.transpose(wt), dest, ys)


def _prep_weights(g_attn_norm, w_in, w_gla_gate_up, b_gla_gate, g_gla_out, g_q, g_k, w_out, g_ffn_norm,
                  w_group_router, b_group_router, w_expert_router, b_expert_router, w_gate, w_up, w_down):
    d = w_in.shape[0]
    o = np.cumsum((GLA_QK_W, GLA_QK_W, GLA_V_W, GLA_V_W, GLA_GATE_RANK, MOBA_W, MOBA_W, MOBA_W))
    wlr = jnp.pad(w_in[:, o[3]:o[4]], ((0, 0), (0, LANES - GLA_GATE_RANK)))
    wup = jnp.pad(w_gla_gate_up, ((0, LANES - GLA_GATE_RANK), (0, 0)))
    wup_hi = wup.astype(BF16)
    wup_lo = (wup - wup_hi.astype(F32)).astype(BF16)
    wr = jnp.zeros((4 * SUBLANES, d), F32)
    wr = wr.at[0:N_GROUPS].set(w_group_router.T).at[SUBLANES:SUBLANES + N_EXPERTS].set(w_expert_router.T)
    br = jnp.zeros((4 * SUBLANES, 1), F32)
    br = br.at[0:N_GROUPS, 0].set(b_group_router).at[SUBLANES:SUBLANES + N_EXPERTS, 0].set(b_expert_router)
    return {
        "g_attn": g_attn_norm[None, :],
        "wg": w_in[:, :o[3]].astype(BF16),
        "wlr": wlr.astype(BF16),
        "wup": jnp.stack([wup_hi, wup_lo]),
        "b_gate": b_gla_gate[None, :],
        "wm": w_in[:, o[4]:].astype(BF16),
        "g_q": jnp.tile(g_q, MOBA_HEADS)[None, :],
        "g_k": jnp.tile(g_k, MOBA_HEADS)[None, :],
        "g_gla_out": g_gla_out[None, :],
        "wo1": w_out[:GLA_V_W].astype(BF16),
        "wo2": w_out[GLA_V_W:].astype(BF16),
        "g_ffn": g_ffn_norm[None, :],
        "wr": wr.astype(BF16),
        "br": br,
        "w_gate": w_gate.astype(BF16),
        "w_up": w_up.astype(BF16),
        "w_down": w_down.astype(BF16),
    }


def _layer(x, positions, s0, gla_chunk, moba_fn, w):
    bsz, l, d = x.shape
    n = bsz * l
    x2d = x.reshape(n, d)
    cos_tab, sin_tab = _rope_tables(positions)
    gq, gk, gv, gr, la, mq, mk, mv = _inproj(x2d, w, cos_tab, sin_tab)

    lp = -(-l // gla_chunk) * gla_chunk
    seq = lambda a: jnp.pad(a.reshape(bsz, l, -1), ((0, 0), (0, lp - l), (0, 0)))
    go, s_new = _gla(seq(gq), seq(gk), seq(gv), seq(la), s0.reshape(bsz, GLA_QK_W, GLA_DV), gla_chunk)
    go = go[:, :l].reshape(n, GLA_V_W)

    mk3 = mk.reshape(bsz, l, MOBA_W)
    mv3 = mv.reshape(bsz, l, MOBA_W)
    mo = moba_fn(mq.reshape(bsz, l, MOBA_W), mk3, mv3).reshape(n, MOBA_W)

    h, xn, eid, wt, rank, counts = _outproj(x2d, go, gr, mo, w)
    y = _moe(h, xn, eid, wt, rank, counts, w)
    return (y.reshape(bsz, l, d), mk3.reshape(bsz, l, MOBA_HEADS, MOBA_HD), mv3.reshape(bsz, l, MOBA_HEADS, MOBA_HD),
            s_new.reshape(bsz, GLA_HEADS, GLA_DK, GLA_DV))


def kernel(x_prompt, x_sample, cache_k, cache_v, state_gla, page_table, g_attn_norm, w_in, w_gla_gate_up, b_gla_gate,
           g_gla_out, g_q, g_k, w_out, g_ffn_norm, w_group_router, b_group_router, w_expert_router, b_expert_router,
           w_gate, w_up, w_down):
    depth = w_in.shape[0]
    bsz, l, _ = x_prompt.shape
    dbs, t_new, _ = x_sample.shape
    n_phys, page = cache_k.shape[1], cache_k.shape[2]
    past = page_table.shape[1] * page
    assert l % ROW_TILE == 0 and (dbs * t_new) % ROW_TILE == 0 and ROW_TILE % t_new == 0

    pos_p = np.arange(l)
    pos_s = past + (np.arange(ROW_TILE) % t_new)
    chunk_s = max(SUBLANES, 1 << (t_new - 1).bit_length())
    cache_kt = jnp.transpose(cache_k, (0, 1, 3, 4, 2))
    cache_vt = jnp.transpose(cache_v, (0, 1, 3, 4, 2))

    hp, hs = x_prompt, x_sample
    outs = [[] for _ in range(6)]
    for li in range(depth):
        w = _prep_weights(g_attn_norm[li], w_in[li], w_gla_gate_up[li], b_gla_gate[li], g_gla_out[li], g_q[li],
                          g_k[li], w_out[li], g_ffn_norm[li], w_group_router[li], b_group_router[li],
                          w_expert_router[li], b_expert_router[li], w_gate[li], w_up[li], w_down[li])
        s0 = jnp.zeros((bsz, GLA_HEADS, GLA_DK, GLA_DV), F32)
        hp, kp, vp, sp = _layer(hp, pos_p, s0, min(GLA_CHUNK, l), _moba_prompt, w)

        moba_s = lambda q, k, v, li=li: _moba_sample(q, k, v, cache_kt, cache_vt, li, page_table)
        hs, ks, vs, ss = _layer(hs, pos_s, state_gla[li].astype(F32), chunk_s, moba_s, w)
        for lst, val in zip(outs, (kp, vp, sp.astype(state_gla.dtype), ks, vs, ss.astype(state_gla.dtype))):
            lst.append(val)

    return (hp, hs) + tuple(jnp.stack(o) for o in outs)
```

```python
import functools
import math

import jax
import jax.numpy as jnp
import numpy as np
from jax import lax
from jax.experimental import pallas as pl
from jax.experimental.pallas import tpu as pltpu

GLA_HEADS = 4
GLA_DK = 64
GLA_DV = 128
GLA_GATE_RANK = 16
GLA_TAU = 16.0
MOBA_HEADS = 8
MOBA_HD = 64
MOBA_BLOCK = 256
MOBA_TOPK = 3
ROPE_THETA = 10000.0
N_GROUPS = 4
EXPERTS_PER_GROUP = 4
N_EXPERTS = N_GROUPS * EXPERTS_PER_GROUP
EXPERT_TOPK = 2
EPS = 1e-6

GLA_QK_W = GLA_HEADS * GLA_DK
GLA_V_W = GLA_HEADS * GLA_DV
MOBA_W = MOBA_HEADS * MOBA_HD

LANES = 128
SUBLANES = 8
VMEM_LIMIT_BYTES = 56 * 1024 * 1024

ROW_TILE = 256
GLA_CHUNK = 256
MOE_BLOCK = 256
SAMPLE_PAGES_PER_STEP = 8
ROW_DMA_UNROLL = 8
MOBA_GROUP = 8
F32 = jnp.float32
BF16 = jnp.bfloat16
NEG = -1e30
LOG2E = math.log2(math.e)
V_AUG_ROWS = MOBA_HD + 16


def _dot(a, b):
    return jnp.dot(a, b, preferred_element_type=F32)


def _dot_nt(a, b):
    return lax.dot_general(a, b, (((1,), (1,)), ((), ())), preferred_element_type=F32)


def _dot_tn(a, b):
    return lax.dot_general(a, b, (((0,), (0,)), ((), ())), preferred_element_type=F32)


def _split2(x):
    hi = x.astype(BF16)
    lo = (x - hi.astype(F32)).astype(BF16)
    return hi, lo


def _params(n_axes):
    return pltpu.CompilerParams(dimension_semantics=("arbitrary",) * n_axes,
                                vmem_limit_bytes=VMEM_LIMIT_BYTES)


def _head_norm(y, g):
    lo = lax.broadcasted_iota(jnp.int32, (1, LANES), 1) < MOBA_HD
    outs = []
    for p in range(MOBA_W // LANES):
        yp = y[:, p * LANES:(p + 1) * LANES]
        sq = yp * yp
        s_lo = jnp.sum(jnp.where(lo, sq, 0.0), axis=-1, keepdims=True)
        s_hi = jnp.sum(jnp.where(lo, 0.0, sq), axis=-1, keepdims=True)
        ms = jnp.where(lo, s_lo, s_hi) * (1.0 / MOBA_HD)
        outs.append(yp * lax.rsqrt(ms + EPS))
    return jnp.concatenate(outs, axis=1) * g


def _rope(y, cos, sin_signed):
    n = y.shape[1]
    first = (lax.broadcasted_iota(jnp.int32, (1, n), 1) % MOBA_HD) < (MOBA_HD // 2)
    rot = jnp.where(first, pltpu.roll(y, n - MOBA_HD // 2, 1), pltpu.roll(y, MOBA_HD // 2, 1))
    return y * cos + rot * sin_signed


def _inproj_kernel(x_ref, gin_ref, wg_ref, wlr_ref, wup_ref, bgate_ref, wm_ref, gq_ref, gk_ref, cos_ref, sin_ref,
                   oq_ref, ok_ref, ov_ref, or_ref, ola_ref, omq_ref, omk_ref, omv_ref):
    x = x_ref[...]
    ms = jnp.mean(x * x, axis=-1, keepdims=True)
    xn = (x * lax.rsqrt(ms + EPS) * gin_ref[...]).astype(BF16)

    yg = _dot(xn, wg_ref[...])
    oq_ref[...] = yg[:, :GLA_QK_W] * (GLA_DK ** -0.5)
    ok_ref[...] = yg[:, GLA_QK_W:2 * GLA_QK_W]
    ov_ref[...] = yg[:, 2 * GLA_QK_W:2 * GLA_QK_W + GLA_V_W]
    or_ref[...] = yg[:, 2 * GLA_QK_W + GLA_V_W:]

    lr = _dot(xn, wlr_ref[...])
    lr_hi, lr_lo = _split2(lr)
    wu_hi = wup_ref[0]
    wu_lo = wup_ref[1]
    z = _dot(lr_hi, wu_hi) + _dot(lr_lo, wu_hi) + _dot(lr_hi, wu_lo) + bgate_ref[...]
    ola_ref[...] = (jnp.minimum(z, 0.0) - jnp.log(1.0 + jnp.exp(-jnp.abs(z)))) * (1.0 / GLA_TAU)

    ym = _dot(xn, wm_ref[...])
    reps = MOBA_W // LANES
    cos = jnp.concatenate([cos_ref[...]] * reps, axis=1)
    sin = jnp.concatenate([sin_ref[...]] * reps, axis=1)
    omq_ref[...] = _rope(_head_norm(ym[:, :MOBA_W], gq_ref[...]), cos, sin)
    omk_ref[...] = _rope(_head_norm(ym[:, MOBA_W:2 * MOBA_W], gk_ref[...]), cos, sin)
    omv_ref[...] = ym[:, 2 * MOBA_W:]


def _inproj(x2d, w, cos_tab, sin_tab):
    n, d = x2d.shape
    t = ROW_TILE
    assert n % t == 0 and cos_tab.shape[0] % t == 0
    n_pos_tiles = cos_tab.shape[0] // t
    row = lambda i: (i, 0)
    fixed = lambda i: (0, 0)
    widths = (GLA_QK_W, GLA_QK_W, GLA_V_W, GLA_V_W, GLA_QK_W, MOBA_W, MOBA_W, MOBA_W)
    return pl.pallas_call(
        _inproj_kernel,
        grid=(n // t,),
        in_specs=[
            pl.BlockSpec((t, d), row),
            pl.BlockSpec((1, d), fixed),
            pl.BlockSpec(w["wg"].shape, fixed),
            pl.BlockSpec(w["wlr"].shape, fixed),
            pl.BlockSpec(w["wup"].shape, lambda i: (0, 0, 0)),
            pl.BlockSpec((1, GLA_QK_W), fixed),
            pl.BlockSpec(w["wm"].shape, fixed),
            pl.BlockSpec((1, MOBA_W), fixed),
            pl.BlockSpec((1, MOBA_W), fixed),
            pl.BlockSpec((t, LANES), lambda i: (i % n_pos_tiles, 0)),
            pl.BlockSpec((t, LANES), lambda i: (i % n_pos_tiles, 0)),
        ],
        out_specs=[pl.BlockSpec((t, wd), row) for wd in widths],
        out_shape=[jax.ShapeDtypeStruct((n, wd), F32) for wd in widths],
        compiler_params=_params(1),
        name="inproj",
    )(x2d, w["g_attn"], w["wg"], w["wlr"], w["wup"], w["b_gate"], w["wm"], w["g_q"], w["g_k"], cos_tab, sin_tab)


def _rope_tables(positions):
    half = MOBA_HD // 2
    inv = ROPE_THETA ** (-np.arange(half, dtype=np.float64) / half)
    ang = positions.astype(np.float64)[:, None] * inv[None, :]
    cos = np.cos(ang)
    sin = np.sin(ang)
    cos_h = np.concatenate([cos, cos], axis=1)
    sin_h = np.concatenate([-sin, sin], axis=1)
    reps = LANES // MOBA_HD
    return (jnp.asarray(np.tile(cos_h, (1, reps)), F32), jnp.asarray(np.tile(sin_h, (1, reps)), F32))


def _gla_level_matrix(c):
    t = np.arange(c)[:, None]
    j = np.arange(c)[None, :]
    mats = [(j <= t)]
    m = c // 2
    while m >= 1:
        base = (t // (2 * m)) * (2 * m)
        ref = base + m - 1
        upper = (t - base) >= m
        mats.append(np.where(upper, (j > ref) & (j <= t), (j > t) & (j <= ref)))
        m //= 2
    return jnp.asarray(np.concatenate(mats, axis=0).astype(np.float32), BF16)


def _gla_kernel(q_ref, k_ref, v_ref, la_ref, s0_ref, lvl_ref, o_ref, sout_ref, s_ref, *, c):
    ci = pl.program_id(1)
    n_lev = int(math.log2(c))
    pair_w = 2 * GLA_DK

    @pl.when(ci == 0)
    def _():
        s_ref[...] = s0_ref[...]

    la = la_ref[...]
    la_hi, la_lo = _split2(la)
    lvl = lvl_ref[...]
    x_all = _dot(lvl, la_hi) + _dot(lvl, la_lo)
    b = x_all[0:c]
    b_last = b[c - 1:c, :]
    q = q_ref[...]
    k = k_ref[...]
    qe = q * jnp.exp(b)
    kd = (k * jnp.exp(b_last - b)).astype(BF16)

    ti = lax.broadcasted_iota(jnp.int32, (c, c), 0)
    si = lax.broadcasted_iota(jnp.int32, (c, c), 1)
    trow = lax.broadcasted_iota(jnp.int32, (c, 1), 0)
    lane = lax.broadcasted_iota(jnp.int32, (1, pair_w), 1)
    ones_cw = jnp.ones((c, pair_w), BF16)
    srow = lax.broadcasted_iota(jnp.int32, (pair_w, 1), 0)

    for p in range(GLA_HEADS // 2):
        cols = slice(p * pair_w, (p + 1) * pair_w)
        q_p = q[:, cols]
        k_p = k[:, cols]
        qe_p = qe[:, cols]
        head_lanes = (lane < GLA_DK, lane >= GLA_DK)
        s_pair = s_ref[cols, :]
        s_pair_b = s_pair.astype(BF16)

        a = [jnp.where(ti == si, _dot_nt(jnp.where(hm, q_p, 0.0).astype(BF16), k_p.astype(BF16)), 0.0)
             for hm in head_lanes]
        for lev in range(n_lev):
            m_log = n_lev - 1 - lev
            e = jnp.exp(x_all[(lev + 1) * c:(lev + 2) * c, cols])
            upper = ((trow >> m_log) & 1) == 1
            kt = jnp.where(upper, 0.0, k_p * e).astype(BF16)
            qt = jnp.where(upper, q_p * e, 0.0)
            same = (ti >> (m_log + 1)) == (si >> (m_log + 1))
            for hh in range(2):
                al = _dot_nt(jnp.where(head_lanes[hh], qt, 0.0).astype(BF16), kt)
                a[hh] = a[hh] + jnp.where(same, al, 0.0)

        u = []
        for hh in range(2):
            h = 2 * p + hh
            v_h = v_ref[:, h * GLA_DV:(h + 1) * GLA_DV].astype(BF16)
            o_h = _dot(a[hh].astype(BF16), v_h) + _dot(jnp.where(head_lanes[hh], qe_p, 0.0).astype(BF16), s_pair_b)
            o_ref[:, h * GLA_DV:(h + 1) * GLA_DV] = o_h
            u.append(_dot_tn(kd[:, cols], v_h))
        bl_rows = _dot_tn(la_hi[:, cols], ones_cw) + _dot_tn(la_lo[:, cols], ones_cw)
        s_ref[cols, :] = jnp.exp(bl_rows[:, :GLA_DV]) * s_pair + jnp.where(srow < GLA_DK, u[0], u[1])

    @pl.when(ci == pl.num_programs(1) - 1)
    def _():
        sout_ref[...] = s_ref[...]


def _gla(q, k, v, la, s0, c):
    bsz, l, _ = q.shape
    assert l % c == 0 and (c & (c - 1)) == 0 and GLA_DV == 2 * GLA_DK
    lvl = _gla_level_matrix(c)
    tok = lambda b, i: (b, i, 0)
    per_b = lambda b, i: (b, 0, 0)
    return pl.pallas_call(
        functools.partial(_gla_kernel, c=c),
        grid=(bsz, l // c),
        in_specs=[
            pl.BlockSpec((None, c, GLA_QK_W), tok),
            pl.BlockSpec((None, c, GLA_QK_W), tok),
            pl.BlockSpec((None, c, GLA_V_W), tok),
            pl.BlockSpec((None, c, GLA_QK_W), tok),
            pl.BlockSpec((None, GLA_QK_W, GLA_DV), per_b),
            pl.BlockSpec(lvl.shape, lambda b, i: (0, 0)),
        ],
        out_specs=[pl.BlockSpec((None, c, GLA_V_W), tok), pl.BlockSpec((None, GLA_QK_W, GLA_DV), per_b)],
        out_shape=[jax.ShapeDtypeStruct((bsz, l, GLA_V_W), F32), jax.ShapeDtypeStruct((bsz, GLA_QK_W, GLA_DV), F32)],
        scratch_shapes=[pltpu.VMEM((GLA_QK_W, GLA_DV), F32)],
        compiler_params=_params(2),
        name="gla",
    )(q, k, v, la, s0, lvl)


def _select_topk_blocks(gate, n_valid):
    row = lax.broadcasted_iota(jnp.int32, gate.shape, 0)
    g = jnp.where(row < n_valid, gate, NEG)
    sel = jnp.zeros(gate.shape, F32)
    for r in range(MOBA_TOPK):
        mx = jnp.max(g, axis=0, keepdims=True)
        idx = jnp.min(jnp.where(g == mx, row, gate.shape[0]), axis=0, keepdims=True)
        pick = row == jnp.where(r < n_valid, idx, -1)
        sel = jnp.where(pick, 1.0, sel)
        g = jnp.where(pick, NEG, g)
    return sel


def _moba_prompt_kernel(q_ref, k_ref, v_ref, o_ref, kmean_ref, sel_ref, *, nb):
    h = pl.program_id(1)
    qi = pl.program_id(2)
    blk = MOBA_BLOCK

    @pl.when(qi == 0)
    def _():
        kmean_ref[...] = jnp.zeros(kmean_ref.shape, F32)

        def body(n, carry):
            kb = k_ref[n].astype(F32)
            kmean_ref[pl.ds(n, 1), :] = jnp.sum(kb, axis=0, keepdims=True) * (1.0 / blk)
            return carry

        lax.fori_loop(0, nb, body, 0)

    q = q_ref[...]
    row = lax.broadcasted_iota(jnp.int32, (2 * MOBA_HD, blk), 0)
    mine = (row // MOBA_HD) == (h % 2)
    q_pair = jnp.where(mine, jnp.concatenate([q, q], axis=0), jnp.zeros((), BF16))

    km_hi, km_lo = _split2(kmean_ref[...])
    gate = _dot(km_hi, q_pair) + _dot(km_lo, q_pair)
    sel_ref[...] = _select_topk_blocks(gate, qi)

    def update(carry, scores, v_tiles, picked):
        m, acc = carry
        m_new = m
        for s, pk in zip(scores, picked):
            bm = jnp.max(s, axis=0, keepdims=True)
            m_new = jnp.maximum(m_new, bm if pk is None else jnp.where(pk, bm, NEG))
        acc = acc * jnp.exp2(m - m_new)
        for s, v_t, pk in zip(scores, v_tiles, picked):
            c = _dot(v_t, jnp.exp2(s - m_new).astype(BF16))
            acc = acc + (c if pk is None else jnp.where(pk, c, 0.0))
        return m_new, acc

    def past_group(gi, carry):
        scores, v_tiles, picked = [], [], []
        for g in range(MOBA_GROUP):
            n = jnp.minimum(gi * MOBA_GROUP + g, nb - 1)
            scores.append(_dot(k_ref[n], q_pair))
            picked.append(sel_ref[pl.ds(n, 1), :] > 0.5)
            v_tiles.append(v_ref[n])
        return update(carry, scores, v_tiles, picked)

    s = _dot(k_ref[qi], q_pair)
    causal = lax.broadcasted_iota(jnp.int32, (blk, blk), 0) <= lax.broadcasted_iota(jnp.int32, (blk, blk), 1)
    carry = (jnp.full((1, blk), NEG, F32), jnp.zeros((v_ref.shape[1], blk), F32))
    carry = update(carry, [jnp.where(causal, s, NEG)], [v_ref[qi]], [None])
    _, acc = lax.fori_loop(0, (qi + MOBA_GROUP - 1) // MOBA_GROUP, past_group, carry)
    o_ref[...] = jnp.transpose(acc[:MOBA_HD] / acc[MOBA_HD:MOBA_HD + 1])


def _moba_prompt(mq, mk, mv):
    bsz, l, _ = mq.shape
    blk = MOBA_BLOCK
    assert l % blk == 0
    nb = l // blk
    nb_pad = -(-nb // SUBLANES) * SUBLANES
    hp = MOBA_HEADS // 2
    qt = (mq * (MOBA_HD ** -0.5 * LOG2E)).astype(BF16).reshape(bsz, nb, blk, MOBA_HEADS, MOBA_HD)
    qt = qt.transpose(0, 3, 1, 4, 2)
    vt = mv.astype(BF16).reshape(bsz, nb, blk, MOBA_HEADS, MOBA_HD).transpose(0, 3, 1, 4, 2)
    ones_rows = jnp.zeros((bsz, MOBA_HEADS, nb, V_AUG_ROWS - MOBA_HD, blk), BF16).at[:, :, :, 0, :].set(1.0)
    vt = jnp.concatenate([vt, ones_rows], axis=3)
    kp = mk.astype(BF16).reshape(bsz, nb, blk, hp, 2 * MOBA_HD).transpose(0, 3, 1, 2, 4)
    out = pl.pallas_call(
        functools.partial(_moba_prompt_kernel, nb=nb),
        grid=(bsz, MOBA_HEADS, nb),
        in_specs=[
            pl.BlockSpec((None, None, None, MOBA_HD, blk), lambda b, h, i: (b, h, i, 0, 0)),
            pl.BlockSpec((None, None, nb, blk, 2 * MOBA_HD), lambda b, h, i: (b, h // 2, 0, 0, 0)),
            pl.BlockSpec((None, None, nb, V_AUG_ROWS, blk), lambda b, h, i: (b, h, 0, 0, 0)),
        ],
        out_specs=pl.BlockSpec((None, None, blk, MOBA_HD), lambda b, h, i: (b, h, i, 0)),
        out_shape=jax.ShapeDtypeStruct((bsz, MOBA_HEADS, l, MOBA_HD), F32),
        scratch_shapes=[pltpu.VMEM((nb_pad, 2 * MOBA_HD), F32), pltpu.VMEM((nb_pad, blk), F32)],
        compiler_params=_params(3),
        name="moba_prompt",
    )(qt, kp, vt)
    return out.transpose(0, 2, 1, 3).reshape(bsz, l, MOBA_W)


def _select_topk_lanes(gate, n_valid):
    lane = lax.broadcasted_iota(jnp.int32, gate.shape, 1)
    g = jnp.where(lane < n_valid, gate, NEG)
    sel = jnp.zeros(gate.shape, F32)
    for r in range(min(MOBA_TOPK, n_valid)):
        mx = jnp.max(g, axis=1, keepdims=True)
        idx = jnp.min(jnp.where(g == mx, lane, gate.shape[1]), axis=1, keepdims=True)
        pick = lane == idx
        sel = jnp.where(pick, 1.0, sel)
        g = jnp.where(pick, NEG, g)
    return sel


def _moba_sample_kernel(pt_ref, qb_ref, kn_ref, vn_ref, *refs, npp, n_steps, n_pages, page, t_new):
    k_refs = refs[:npp]
    v_refs = refs[npp:2 * npp]
    o_ref = refs[2 * npp]
    s_ref, p_ref, acc_ref = refs[2 * npp + 1:]
    j = pl.program_id(1)
    blk = MOBA_BLOCK
    n_blk = (n_pages * page) // blk
    past = n_pages * page
    n_c = qb_ref.shape[0]
    t_pad = n_c // MOBA_HEADS

    def flat(ref):
        return ref[...].reshape(MOBA_W, page)

    @pl.when(j < n_steps)
    def _():
        qb = qb_ref[...]
        for i in range(0, npp, 2):
            w = jnp.concatenate([flat(k_refs[i]), flat(k_refs[i + 1])], axis=1).astype(BF16)
            cols = pl.ds(pl.multiple_of((j * npp + i) * page, 2 * page), 2 * page)
            s_ref[:, cols] = _dot(qb, w)

    @pl.when(j == n_steps - 1)
    def _():
        qb = qb_ref[...]
        lane = lax.broadcasted_iota(jnp.int32, (n_c, LANES), 1)

        def blk_cols(n):
            return pl.ds(pl.multiple_of(n * blk, blk), blk)

        def gate_body(n, gate):
            gsum = jnp.sum(s_ref[:, blk_cols(n)], axis=1, keepdims=True) * (1.0 / blk)
            return jnp.where(lane == n, gsum, gate)

        unroll = math.gcd(n_blk, 8)
        gate = lax.fori_loop(0, n_blk, gate_body, jnp.zeros((n_c, LANES), F32), unroll=unroll)
        sel = _select_topk_lanes(gate, n_blk)

        s_own = _dot(qb, kn_ref[...].astype(BF16))
        key_t = lax.broadcasted_iota(jnp.int32, (n_c, page), 1)
        row_t = lax.broadcasted_iota(jnp.int32, (n_c, page), 0) % t_pad
        s_own = jnp.where((key_t < t_new) & (key_t <= row_t), s_own, NEG)

        def masked(n):
            picked = jnp.sum(jnp.where(lane == n, sel, 0.0), axis=1, keepdims=True) > 0.5
            return jnp.where(picked, s_ref[:, blk_cols(n)], NEG)

        def max_body(n, m):
            return jnp.maximum(m, masked(n))

        m_run = lax.fori_loop(0, n_blk, max_body, jnp.full((n_c, blk), NEG, F32), unroll=unroll)
        m = jnp.maximum(jnp.max(m_run, axis=1, keepdims=True), jnp.max(s_own, axis=1, keepdims=True))

        def exp_body(n, l_run):
            e = jnp.exp(masked(n) - m)
            s_ref[:, blk_cols(n)] = e
            return l_run + e

        l_run = lax.fori_loop(0, n_blk, exp_body, jnp.zeros((n_c, blk), F32), unroll=unroll)
        e_own = jnp.exp(s_own - m)
        inv = 1.0 / (jnp.sum(l_run, axis=1, keepdims=True) + jnp.sum(e_own, axis=1, keepdims=True))

        def norm_body(n, carry):
            p_ref[:, blk_cols(n)] = (s_ref[:, blk_cols(n)] * inv).astype(BF16)
            return carry

        lax.fori_loop(0, n_blk, norm_body, 0, unroll=unroll)
        p_ref[:, past:past + page] = (e_own * inv).astype(BF16)
        acc_ref[...] = jnp.zeros(acc_ref.shape, F32)

    @pl.when(j >= n_steps)
    def _():
        v_cat = jnp.concatenate([flat(v_refs[i]) for i in range(npp)], axis=1).astype(BF16)
        cols = pl.ds(pl.multiple_of((j - n_steps) * (npp * page), npp * page), npp * page)
        acc_ref[...] += _dot_nt(v_cat, p_ref[:, cols])

    @pl.when(j == 2 * n_steps - 1)
    def _():
        o_ref[...] = acc_ref[...] + _dot_nt(vn_ref[...].astype(BF16), p_ref[:, past:past + page])


def _moba_sample(mq, mk, mv, cache_kt, cache_vt, layer, page_table):
    dbs, t_new, _ = mq.shape
    page = cache_kt.shape[-1]
    n_pages = page_table.shape[1]
    npp = SAMPLE_PAGES_PER_STEP
    t_pad = t_new + t_new % 2
    n_c = MOBA_HEADS * t_pad
    assert n_pages % npp == 0 and npp % 2 == 0 and page == LANES
    assert (n_pages * page) % MOBA_BLOCK == 0 and (n_pages * page) // MOBA_BLOCK <= LANES and t_new <= page
    n_steps = n_pages // npp

    q4 = (mq * (MOBA_HD ** -0.5)).reshape(dbs, t_new, MOBA_HEADS, MOBA_HD)
    q4 = jnp.pad(q4, ((0, 0), (0, t_pad - t_new), (0, 0), (0, 0)))
    eye = jnp.eye(MOBA_HEADS, dtype=F32)
    qb = (q4[:, :, :, None, :] * eye[None, None, :, :, None]).transpose(0, 2, 1, 3, 4)
    qb = qb.reshape(dbs, n_c, MOBA_W).astype(BF16)
    kn = jnp.pad(jnp.transpose(mk, (0, 2, 1)), ((0, 0), (0, 0), (0, page - t_new)))
    vn = jnp.pad(jnp.transpose(mv, (0, 2, 1)), ((0, 0), (0, 0), (0, page - t_new)))
    pt_flat = page_table.reshape(-1).astype(jnp.int32)

    def k_map(i):
        return lambda b, j, pt: (layer, pt[b * n_pages + jnp.minimum(j, n_steps - 1) * npp + i], 0, 0, 0)

    def v_map(i):
        return lambda b, j, pt: (layer, pt[b * n_pages + jnp.maximum(j - n_steps, 0) * npp + i], 0, 0, 0)

    per_b = lambda b, j, pt: (b, 0, 0)
    page_block = (None, None, MOBA_HEADS, MOBA_HD, page)
    grid_spec = pltpu.PrefetchScalarGridSpec(
        num_scalar_prefetch=1,
        grid=(dbs, 2 * n_steps),
        in_specs=[pl.BlockSpec((None, n_c, MOBA_W), per_b),
                  pl.BlockSpec((None, MOBA_W, page), per_b),
                  pl.BlockSpec((None, MOBA_W, page), per_b)]
                 + [pl.BlockSpec(page_block, k_map(i)) for i in range(npp)]
                 + [pl.BlockSpec(page_block, v_map(i)) for i in range(npp)],
        out_specs=pl.BlockSpec((None, MOBA_W, n_c), per_b),
        scratch_shapes=[
            pltpu.VMEM((n_c, n_pages * page), F32),
            pltpu.VMEM((n_c, (n_pages + 1) * page), BF16),
            pltpu.VMEM((MOBA_W, n_c), F32),
        ],
    )
    out_t = pl.pallas_call(
        functools.partial(_moba_sample_kernel, npp=npp, n_steps=n_steps, n_pages=n_pages, page=page, t_new=t_new),
        grid_spec=grid_spec,
        out_shape=jax.ShapeDtypeStruct((dbs, MOBA_W, n_c), F32),
        compiler_params=_params(2),
        name="moba_sample",
    )(pt_flat, qb, kn, vn, *([cache_kt] * npp), *([cache_vt] * npp))
    o5 = out_t.reshape(dbs, MOBA_HEADS, MOBA_HD, MOBA_HEADS, t_pad)
    o = jnp.sum(o5 * eye[None, :, None, :, None], axis=3)
    return o.transpose(0, 3, 1, 2)[:, :t_new].reshape(dbs, t_new, MOBA_W)


def _outproj_kernel(x_ref, go_ref, gr_ref, mo_ref, ggo_ref, wo1_ref, wo2_ref, gffn_ref, wr_ref, br_ref,
                    h_ref, xn_ref, eid_ref, wt_ref, rank_ref, cnt_ref, carry_ref):
    i = pl.program_id(0)
    t = x_ref.shape[0]

    @pl.when(i == 0)
    def _():
        carry_ref[...] = jnp.zeros(carry_ref.shape, F32)

    gr = gr_ref[...]
    gate = gr / (1.0 + jnp.exp(-gr))
    parts = []
    for hd in range(GLA_HEADS):
        cols = slice(hd * GLA_DV, (hd + 1) * GLA_DV)
        o = go_ref[:, cols]
        ms = jnp.mean(o * o, axis=-1, keepdims=True)
        parts.append(o * lax.rsqrt(ms + EPS) * ggo_ref[...] * gate[:, cols])
    a1 = jnp.concatenate(parts, axis=1).astype(BF16)
    h = x_ref[...] + _dot(a1, wo1_ref[...]) + _dot(mo_ref[...].astype(BF16), wo2_ref[...])
    h_ref[...] = h

    ms = jnp.mean(h * h, axis=-1, keepdims=True)
    xn = h * lax.rsqrt(ms + EPS) * gffn_ref[...]
    xn_ref[...] = xn
    logits = _dot_nt(wr_ref[...], xn.astype(BF16)) + br_ref[...]

    gl = logits[0:N_GROUPS]
    grow = lax.broadcasted_iota(jnp.int32, gl.shape, 0)
    gmax = jnp.max(gl, axis=0, keepdims=True)
    g_sel = jnp.min(jnp.where(gl == gmax, grow, N_GROUPS), axis=0, keepdims=True)
    gw = 1.0 / jnp.sum(jnp.exp(gl - gmax), axis=0, keepdims=True)

    el = logits[SUBLANES:SUBLANES + N_EXPERTS]
    erow = lax.broadcasted_iota(jnp.int32, el.shape, 0)
    in_grp = (erow // EXPERTS_PER_GROUP) == g_sel
    emax = jnp.max(jnp.where(in_grp, el, NEG), axis=0, keepdims=True)
    ee = jnp.where(in_grp, jnp.exp(el - emax), 0.0)
    ep = ee / jnp.sum(ee, axis=0, keepdims=True)
    cand = jnp.where(in_grp, ep, -1.0)
    p1 = jnp.max(cand, axis=0, keepdims=True)
    i1 = jnp.min(jnp.where(cand == p1, erow, N_EXPERTS), axis=0, keepdims=True)
    cand = jnp.where(erow == i1, -1.0, cand)
    p2 = jnp.max(cand, axis=0, keepdims=True)
    i2 = jnp.min(jnp.where(cand == p2, erow, N_EXPERTS), axis=0, keepdims=True)
    denom = p1 + p2
    zero_i = jnp.zeros((SUBLANES - EXPERT_TOPK, t), jnp.int32)
    zero_f = jnp.zeros((SUBLANES - EXPERT_TOPK, t), F32)
    eid_ref[...] = jnp.concatenate([i1, i2, zero_i], axis=0)
    wt_ref[...] = jnp.concatenate([p1 / denom * gw, p2 / denom * gw, zero_f], axis=0)

    oh1 = erow == i1
    oh2 = erow == i2
    hits = jnp.where(oh1 | oh2, 1.0, 0.0)
    before = (lax.broadcasted_iota(jnp.int32, (t, t), 0) < lax.broadcasted_iota(jnp.int32, (t, t), 1))
    prefix = _dot(hits.astype(BF16), jnp.where(before, 1.0, 0.0).astype(BF16)) + carry_ref[...]
    r1 = jnp.sum(jnp.where(oh1, prefix, 0.0), axis=0, keepdims=True)
    r2 = jnp.sum(jnp.where(oh2, prefix, 0.0), axis=0, keepdims=True)
    rank_ref[...] = jnp.concatenate([r1.astype(jnp.int32), r2.astype(jnp.int32), zero_i], axis=0)
    carry_ref[...] += jnp.sum(hits, axis=1, keepdims=True)
    cnt_ref[...] = jnp.broadcast_to(carry_ref[...], cnt_ref.shape)


def _outproj(x2d, go, gr, mo, w):
    n, d = x2d.shape
    t = ROW_TILE
    assert n % t == 0
    row = lambda i: (i, 0)
    col = lambda i: (0, i)
    fixed = lambda i: (0, 0)
    return pl.pallas_call(
        _outproj_kernel,
        grid=(n // t,),
        in_specs=[
            pl.BlockSpec((t, d), row),
            pl.BlockSpec((t, GLA_V_W), row),
            pl.BlockSpec((t, GLA_V_W), row),
            pl.BlockSpec((t, MOBA_W), row),
            pl.BlockSpec((1, GLA_DV), fixed),
            pl.BlockSpec(w["wo1"].shape, fixed),
            pl.BlockSpec(w["wo2"].shape, fixed),
            pl.BlockSpec((1, d), fixed),
            pl.BlockSpec(w["wr"].shape, fixed),
            pl.BlockSpec(w["br"].shape, fixed),
        ],
        out_specs=[pl.BlockSpec((t, d), row), pl.BlockSpec((t, d), row),
                   pl.BlockSpec((SUBLANES, t), col), pl.BlockSpec((SUBLANES, t), col),
                   pl.BlockSpec((SUBLANES, t), col), pl.BlockSpec((N_EXPERTS, LANES), fixed)],
        out_shape=[jax.ShapeDtypeStruct((n, d), F32), jax.ShapeDtypeStruct((n, d), F32),
                   jax.ShapeDtypeStruct((SUBLANES, n), jnp.int32), jax.ShapeDtypeStruct((SUBLANES, n), F32),
                   jax.ShapeDtypeStruct((SUBLANES, n), jnp.int32), jax.ShapeDtypeStruct((N_EXPERTS, LANES), F32)],
        scratch_shapes=[pltpu.VMEM((N_EXPERTS, 1), F32)],
        compiler_params=_params(1),
        name="outproj",
    )(x2d, go, gr, mo, w["g_gla_out"], w["wo1"], w["wo2"], w["g_ffn"], w["wr"], w["br"])


def _row_copy(src_ref, src_row, dst_ref, dst_row, sem):
    return pltpu.make_async_copy(src_ref.at[pl.ds(src_row, 1)], dst_ref.at[pl.ds(dst_row, 1)], sem)


def _moe_scatter_kernel(dest_ref, x_ref, xs_in_ref, xs_ref, sem):
    del xs_in_ref
    t = x_ref.shape[0]

    def start(r, carry):
        for kk in range(EXPERT_TOPK):
            _row_copy(x_ref, r, xs_ref, dest_ref[kk, r], sem).start(priority=kk % 2)
        return carry

    def wait(r, carry):
        for kk in range(EXPERT_TOPK):
            _row_copy(x_ref, 0, xs_ref, 0, sem).wait()
        return carry

    lax.fori_loop(0, t, start, 0, unroll=ROW_DMA_UNROLL)
    lax.fori_loop(0, t, wait, 0, unroll=ROW_DMA_UNROLL)


def _moe_scatter(xn, dest, n_rows):
    n, d = xn.shape
    t = ROW_TILE
    xs0 = jnp.zeros((n_rows, d), F32)
    return pl.pallas_call(
        _moe_scatter_kernel,
        grid=(n // t,),
        in_specs=[
            pl.BlockSpec((EXPERT_TOPK, t), lambda i: (0, i), memory_space=pltpu.SMEM),
            pl.BlockSpec((t, d), lambda i: (i, 0)),
            pl.BlockSpec(memory_space=pl.ANY),
        ],
        out_specs=pl.BlockSpec(memory_space=pl.ANY),
        out_shape=jax.ShapeDtypeStruct((n_rows, d), F32),
        scratch_shapes=[pltpu.SemaphoreType.DMA(())],
        input_output_aliases={2: 0},
        compiler_params=_params(1),
        name="moe_scatter",
    )(dest, xn, xs0)


def _moe_ffn_kernel(be_ref, nu_ref, xs_ref, wg_ref, wu_ref, wd_ref, ys_ref):
    i = pl.program_id(0)

    @pl.when(i < nu_ref[0])
    def _():
        x = xs_ref[...].astype(BF16)
        g = _dot(x, wg_ref[...])
        u = _dot(x, wu_ref[...])
        hmid = (g / (1.0 + jnp.exp(-g)) * u).astype(BF16)
        ys_ref[...] = _dot(hmid, wd_ref[...])

    @pl.when(i >= nu_ref[0])
    def _():
        ys_ref[...] = jnp.zeros(ys_ref.shape, F32)


def _moe_ffn(xs, blk_expert, n_used, wg, wu, wd):
    r, d = xs.shape
    de = wg.shape[2]
    n_blk = r // MOE_BLOCK
    rows = lambda i, be, nu: (jnp.minimum(i, nu[0] - 1), 0)
    wsel = lambda i, be, nu: (be[jnp.minimum(i, nu[0] - 1)], 0, 0)
    grid_spec = pltpu.PrefetchScalarGridSpec(
        num_scalar_prefetch=2,
        grid=(n_blk,),
        in_specs=[
            pl.BlockSpec((MOE_BLOCK, d), rows),
            pl.BlockSpec((None, d, de), wsel),
            pl.BlockSpec((None, d, de), wsel),
            pl.BlockSpec((None, de, d), wsel),
        ],
        out_specs=pl.BlockSpec((MOE_BLOCK, d), lambda i, be, nu: (i, 0)),
    )
    return pl.pallas_call(
        _moe_ffn_kernel,
        grid_spec=grid_spec,
        out_shape=jax.ShapeDtypeStruct((r, d), F32),
        compiler_params=_params(1),
        name="moe_ffn",
    )(blk_expert, n_used, xs, wg, wu, wd)


def _moe_combine_kernel(dest_ref, h_ref, wt_ref, ys_ref, y_ref, buf_ref, sem):
    t = h_ref.shape[0]

    def start(r, carry):
        for kk in range(EXPERT_TOPK):
            _row_copy(ys_ref, dest_ref[kk, r], buf_ref.at[kk], r, sem).start(priority=kk % 2)
        return carry

    def wait(r, carry):
        for kk in range(EXPERT_TOPK):
            _row_copy(ys_ref, 0, buf_ref.at[kk], 0, sem).wait()
        return carry

    lax.fori_loop(0, t, start, 0, unroll=ROW_DMA_UNROLL)
    lax.fori_loop(0, t, wait, 0, unroll=ROW_DMA_UNROLL)
    y = h_ref[...]
    for kk in range(EXPERT_TOPK):
        y = y + buf_ref[kk] * wt_ref[:, kk:kk + 1]
    y_ref[...] = y


def _moe_combine(h, wt_rows, dest, ys):
    n, d = h.shape
    t = ROW_TILE
    return pl.pallas_call(
        _moe_combine_kernel,
        grid=(n // t,),
        in_specs=[
            pl.BlockSpec((EXPERT_TOPK, t), lambda i: (0, i), memory_space=pltpu.SMEM),
            pl.BlockSpec((t, d), lambda i: (i, 0)),
            pl.BlockSpec((t, SUBLANES), lambda i: (i, 0)),
            pl.BlockSpec(memory_space=pl.ANY),
        ],
        out_specs=pl.BlockSpec((t, d), lambda i: (i, 0)),
        out_shape=jax.ShapeDtypeStruct((n, d), F32),
        scratch_shapes=[pltpu.VMEM((EXPERT_TOPK, t, d), F32), pltpu.SemaphoreType.DMA(())],
        compiler_params=_params(1),
        name="moe_combine",
    )(dest, h, wt_rows, ys)


def _moe(h, xn, eid, wt, rank, counts, w):
    n, d = h.shape
    blk = MOE_BLOCK
    n_blk = -(-(n * EXPERT_TOPK) // blk) + N_EXPERTS
    cnt = counts[:, 0].astype(jnp.int32)
    padded = (cnt + blk - 1) // blk * blk
    pend = jnp.cumsum(padded)
    pstart = pend - padded
    dest = rank[:EXPERT_TOPK]
    for e in range(N_EXPERTS):
        dest = dest + jnp.where(eid[:EXPERT_TOPK] == e, pstart[e], 0)
    blk_expert = jnp.minimum(jnp.searchsorted(pend, jnp.arange(n_blk) * blk, side="right"),
                             N_EXPERTS - 1).astype(jnp.int32)
    n_used = (pend[-1:] // blk).astype(jnp.int32)

    xs = _moe_scatter(xn, dest, n_blk * blk)
    ys = _moe_ffn(xs, blk_expert, n_used, w["w_gate"], w["w_up"], w["w_down"])
    return _moe_combine(h, jnp.transpose(wt), dest, ys)


def _prep_weights(g_attn_norm, w_in, w_gla_gate_up, b_gla_gate, g_gla_out, g_q, g_k, w_out, g_ffn_norm,
                  w_group_router, b_group_router, w_expert_router, b_expert_router, w_gate, w_up, w_down):
    d = w_in.shape[0]
    o = np.cumsum((GLA_QK_W, GLA_QK_W, GLA_V_W, GLA_V_W, GLA_GATE_RANK, MOBA_W, MOBA_W, MOBA_W))
    wlr = jnp.pad(w_in[:, o[3]:o[4]], ((0, 0), (0, LANES - GLA_GATE_RANK)))
    wup = jnp.pad(w_gla_gate_up, ((0, LANES - GLA_GATE_RANK), (0, 0)))
    wup_hi = wup.astype(BF16)
    wup_lo = (wup - wup_hi.astype(F32)).astype(BF16)
    wr = jnp.zeros((4 * SUBLANES, d), F32)
    wr = wr.at[0:N_GROUPS].set(w_group_router.T).at[SUBLANES:SUBLANES + N_EXPERTS].set(w_expert_router.T)
    br = jnp.zeros((4 * SUBLANES, 1), F32)
    br = br.at[0:N_GROUPS, 0].set(b_group_router).at[SUBLANES:SUBLANES + N_EXPERTS, 0].set(b_expert_router)
    return {
        "g_attn": g_attn_norm[None, :],
        "wg": w_in[:, :o[3]].astype(BF16),
        "wlr": wlr.astype(BF16),
        "wup": jnp.stack([wup_hi, wup_lo]),
        "b_gate": b_gla_gate[None, :],
        "wm": w_in[:, o[4]:].astype(BF16),
        "g_q": jnp.tile(g_q, MOBA_HEADS)[None, :],
        "g_k": jnp.tile(g_k, MOBA_HEADS)[None, :],
        "g_gla_out": g_gla_out[None, :],
        "wo1": w_out[:GLA_V_W].astype(BF16),
        "wo2": w_out[GLA_V_W:].astype(BF16),
        "g_ffn": g_ffn_norm[None, :],
        "wr": wr.astype(BF16),
        "br": br,
        "w_gate": w_gate.astype(BF16),
        "w_up": w_up.astype(BF16),
        "w_down": w_down.astype(BF16),
    }


def _layer(x, positions, s0, gla_chunk, moba_fn, w):
    bsz, l, d = x.shape
    n = bsz * l
    x2d = x.reshape(n, d)
    cos_tab, sin_tab = _rope_tables(positions)
    gq, gk, gv, gr, la, mq, mk, mv = _inproj(x2d, w, cos_tab, sin_tab)

    lp = -(-l // gla_chunk) * gla_chunk
    seq = lambda a: jnp.pad(a.reshape(bsz, l, -1), ((0, 0), (0, lp - l), (0, 0)))
    go, s_new = _gla(seq(gq), seq(gk), seq(gv), seq(la), s0.reshape(bsz, GLA_QK_W, GLA_DV), gla_chunk)
    go = go[:, :l].reshape(n, GLA_V_W)

    mk3 = mk.reshape(bsz, l, MOBA_W)
    mv3 = mv.reshape(bsz, l, MOBA_W)
    mo = moba_fn(mq.reshape(bsz, l, MOBA_W), mk3, mv3).reshape(n, MOBA_W)

    h, xn, eid, wt, rank, counts = _outproj(x2d, go, gr, mo, w)
    y = _moe(h, xn, eid, wt, rank, counts, w)
    return (y.reshape(bsz, l, d), mk3.reshape(bsz, l, MOBA_HEADS, MOBA_HD), mv3.reshape(bsz, l, MOBA_HEADS, MOBA_HD),
            s_new.reshape(bsz, GLA_HEADS, GLA_DK, GLA_DV))


def kernel(x_prompt, x_sample, cache_k, cache_v, state_gla, page_table, g_attn_norm, w_in, w_gla_gate_up, b_gla_gate,
           g_gla_out, g_q, g_k, w_out, g_ffn_norm, w_group_router, b_group_router, w_expert_router, b_expert_router,
           w_gate, w_up, w_down):
    depth = w_in.shape[0]
    bsz, l, _ = x_prompt.shape
    dbs, t_new, _ = x_sample.shape
    n_phys, page = cache_k.shape[1], cache_k.shape[2]
    past = page_table.shape[1] * page
    assert l % ROW_TILE == 0 and (dbs * t_new) % ROW_TILE == 0 and ROW_TILE % t_new == 0

    pos_p = np.arange(l)
    pos_s = past + (np.arange(ROW_TILE) % t_new)
    chunk_s = max(SUBLANES, 1 << (t_new - 1).bit_length())
    cache_kt = jnp.transpose(cache_k, (0, 1, 3, 4, 2))
    cache_vt = jnp.transpose(cache_v, (0, 1, 3, 4, 2))

    hp, hs = x_prompt, x_sample
    outs = [[] for _ in range(6)]
    for li in range(depth):
        w = _prep_weights(g_attn_norm[li], w_in[li], w_gla_gate_up[li], b_gla_gate[li], g_gla_out[li], g_q[li],
                          g_k[li], w_out[li], g_ffn_norm[li], w_group_router[li], b_group_router[li],
                          w_expert_router[li], b_expert_router[li], w_gate[li], w_up[li], w_down[li])
        s0 = jnp.zeros((bsz, GLA_HEADS, GLA_DK, GLA_DV), F32)
        hp, kp, vp, sp = _layer(hp, pos_p, s0, min(GLA_CHUNK, l), _moba_prompt, w)

        moba_s = lambda q, k, v, li=li: _moba_sample(q, k, v, cache_kt, cache_vt, li, page_table)
        hs, ks, vs, ss = _layer(hs, pos_s, state_gla[li].astype(F32), chunk_s, moba_s, w)
        for lst, val in zip(outs, (kp, vp, sp.astype(state_gla.dtype), ks, vs, ss.astype(state_gla.dtype))):
            lst.append(val)

    return (hp, hs) + tuple(jnp.stack(o) for o in outs)
```

```python
import functools
import math

import jax
import jax.numpy as jnp
import numpy as np
from jax import lax
from jax.experimental import pallas as pl
from jax.experimental.pallas import tpu as pltpu

GLA_HEADS = 4
GLA_DK = 64
GLA_DV = 128
GLA_GATE_RANK = 16
GLA_TAU = 16.0
MOBA_HEADS = 8
MOBA_HD = 64
MOBA_BLOCK = 256
MOBA_TOPK = 3
ROPE_THETA = 10000.0
N_GROUPS = 4
EXPERTS_PER_GROUP = 4
N_EXPERTS = N_GROUPS * EXPERTS_PER_GROUP
EXPERT_TOPK = 2
EPS = 1e-6

GLA_QK_W = GLA_HEADS * GLA_DK
GLA_V_W = GLA_HEADS * GLA_DV
MOBA_W = MOBA_HEADS * MOBA_HD

LANES = 128
SUBLANES = 8
VMEM_LIMIT_BYTES = 56 * 1024 * 1024

ROW_TILE = 256
GLA_CHUNK = 256
MOE_BLOCK = 256
SAMPLE_PAGES_PER_STEP = 8
ROW_DMA_UNROLL = 8
MOBA_GROUP = 4
F32 = jnp.float32
BF16 = jnp.bfloat16
NEG = -1e30
LOG2E = math.log2(math.e)


def _dot(a, b):
    return jnp.dot(a, b, preferred_element_type=F32)


def _dot_nt(a, b):
    return lax.dot_general(a, b, (((1,), (1,)), ((), ())), preferred_element_type=F32)


def _dot_tn(a, b):
    return lax.dot_general(a, b, (((0,), (0,)), ((), ())), preferred_element_type=F32)


def _split2(x):
    hi = x.astype(BF16)
    lo = (x - hi.astype(F32)).astype(BF16)
    return hi, lo


def _params(n_axes):
    return pltpu.CompilerParams(dimension_semantics=("arbitrary",) * n_axes,
                                vmem_limit_bytes=VMEM_LIMIT_BYTES)


def _head_norm(y, g):
    lo = lax.broadcasted_iota(jnp.int32, (1, LANES), 1) < MOBA_HD
    outs = []
    for p in range(MOBA_W // LANES):
        yp = y[:, p * LANES:(p + 1) * LANES]
        sq = yp * yp
        s_lo = jnp.sum(jnp.where(lo, sq, 0.0), axis=-1, keepdims=True)
        s_hi = jnp.sum(jnp.where(lo, 0.0, sq), axis=-1, keepdims=True)
        ms = jnp.where(lo, s_lo, s_hi) * (1.0 / MOBA_HD)
        outs.append(yp * lax.rsqrt(ms + EPS))
    return jnp.concatenate(outs, axis=1) * g


def _rope(y, cos, sin_signed):
    n = y.shape[1]
    first = (lax.broadcasted_iota(jnp.int32, (1, n), 1) % MOBA_HD) < (MOBA_HD // 2)
    rot = jnp.where(first, pltpu.roll(y, n - MOBA_HD // 2, 1), pltpu.roll(y, MOBA_HD // 2, 1))
    return y * cos + rot * sin_signed


def _inproj_kernel(x_ref, gin_ref, wg_ref, wlr_ref, wup_ref, bgate_ref, wm_ref, gq_ref, gk_ref, cos_ref, sin_ref,
                   oq_ref, ok_ref, ov_ref, or_ref, ola_ref, *moba_refs, transposed):
    x = x_ref[...]
    ms = jnp.mean(x * x, axis=-1, keepdims=True)
    xn = (x * lax.rsqrt(ms + EPS) * gin_ref[...]).astype(BF16)

    yg = _dot(xn, wg_ref[...])
    oq_ref[...] = yg[:, :GLA_QK_W] * (GLA_DK ** -0.5)
    ok_ref[...] = yg[:, GLA_QK_W:2 * GLA_QK_W]
    ov_ref[...] = yg[:, 2 * GLA_QK_W:2 * GLA_QK_W + GLA_V_W]
    or_ref[...] = yg[:, 2 * GLA_QK_W + GLA_V_W:]

    lr = _dot(xn, wlr_ref[...])
    lr_hi, lr_lo = _split2(lr)
    wu_hi = wup_ref[0]
    wu_lo = wup_ref[1]
    z = _dot(lr_hi, wu_hi) + _dot(lr_lo, wu_hi) + _dot(lr_hi, wu_lo) + bgate_ref[...]
    ola_ref[...] = (jnp.minimum(z, 0.0) - jnp.log(1.0 + jnp.exp(-jnp.abs(z)))) * (1.0 / GLA_TAU)

    ym = _dot(xn, wm_ref[...])
    reps = MOBA_W // LANES
    cos = jnp.concatenate([cos_ref[...]] * reps, axis=1)
    sin = jnp.concatenate([sin_ref[...]] * reps, axis=1)
    mq = _rope(_head_norm(ym[:, :MOBA_W], gq_ref[...]), cos, sin)
    mk = _rope(_head_norm(ym[:, MOBA_W:2 * MOBA_W], gk_ref[...]), cos, sin)
    mv = ym[:, 2 * MOBA_W:]
    if transposed:
        qt_ref, kt_ref, vt_ref, kn_ref = moba_refs
        qt_ref[...] = jnp.transpose(mq * (MOBA_HD ** -0.5 * LOG2E)).astype(BF16)
        kt_ref[...] = jnp.transpose(mk)
        vt_ref[...] = jnp.transpose(mv)
        kn_ref[...] = mk.astype(BF16)
    else:
        omq_ref, omk_ref, omv_ref = moba_refs
        omq_ref[...] = mq
        omk_ref[...] = mk
        omv_ref[...] = mv


def _inproj(x2d, w, cos_tab, sin_tab, seq_len=None):
    n, d = x2d.shape
    t = ROW_TILE
    assert n % t == 0 and cos_tab.shape[0] % t == 0
    n_pos_tiles = cos_tab.shape[0] // t
    row = lambda i: (i, 0)
    fixed = lambda i: (0, 0)
    widths = (GLA_QK_W, GLA_QK_W, GLA_V_W, GLA_V_W, GLA_QK_W)
    out_specs = [pl.BlockSpec((t, wd), row) for wd in widths]
    out_shape = [jax.ShapeDtypeStruct((n, wd), F32) for wd in widths]
    if seq_len is None:
        out_specs += [pl.BlockSpec((t, MOBA_W), row)] * 3
        out_shape += [jax.ShapeDtypeStruct((n, MOBA_W), F32)] * 3
    else:
        assert seq_len % t == 0
        tiles = seq_len // t
        bsz = n // seq_len
        tok_lanes = lambda i: (i // tiles, 0, i % tiles)
        out_specs += [pl.BlockSpec((None, MOBA_W, t), tok_lanes)] * 3
        out_specs += [pl.BlockSpec((None, t, MOBA_W), lambda i: (i // tiles, i % tiles, 0))]
        out_shape += [jax.ShapeDtypeStruct((bsz, MOBA_W, seq_len), dt) for dt in (BF16, F32, F32)]
        out_shape += [jax.ShapeDtypeStruct((bsz, seq_len, MOBA_W), BF16)]
    return pl.pallas_call(
        functools.partial(_inproj_kernel, transposed=seq_len is not None),
        grid=(n // t,),
        in_specs=[
            pl.BlockSpec((t, d), row),
            pl.BlockSpec((1, d), fixed),
            pl.BlockSpec(w["wg"].shape, fixed),
            pl.BlockSpec(w["wlr"].shape, fixed),
            pl.BlockSpec(w["wup"].shape, lambda i: (0, 0, 0)),
            pl.BlockSpec((1, GLA_QK_W), fixed),
            pl.BlockSpec(w["wm"].shape, fixed),
            pl.BlockSpec((1, MOBA_W), fixed),
            pl.BlockSpec((1, MOBA_W), fixed),
            pl.BlockSpec((t, LANES), lambda i: (i % n_pos_tiles, 0)),
            pl.BlockSpec((t, LANES), lambda i: (i % n_pos_tiles, 0)),
        ],
        out_specs=out_specs,
        out_shape=out_shape,
        compiler_params=_params(1),
        name="inproj",
    )(x2d, w["g_attn"], w["wg"], w["wlr"], w["wup"], w["b_gate"], w["wm"], w["g_q"], w["g_k"], cos_tab, sin_tab)


def _rope_tables(positions):
    half = MOBA_HD // 2
    inv = ROPE_THETA ** (-np.arange(half, dtype=np.float64) / half)
    ang = positions.astype(np.float64)[:, None] * inv[None, :]
    cos = np.cos(ang)
    sin = np.sin(ang)
    cos_h = np.concatenate([cos, cos], axis=1)
    sin_h = np.concatenate([-sin, sin], axis=1)
    reps = LANES // MOBA_HD
    return (jnp.asarray(np.tile(cos_h, (1, reps)), F32), jnp.asarray(np.tile(sin_h, (1, reps)), F32))


def _gla_level_matrix(c):
    t = np.arange(c)[:, None]
    j = np.arange(c)[None, :]
    mats = [(j <= t)]
    m = c // 2
    while m >= 1:
        base = (t // (2 * m)) * (2 * m)
        ref = base + m - 1
        upper = (t - base) >= m
        mats.append(np.where(upper, (j > ref) & (j <= t), (j > t) & (j <= ref)))
        m //= 2
    return jnp.asarray(np.concatenate(mats, axis=0).astype(np.float32), BF16)


def _gla_kernel(q_ref, k_ref, v_ref, la_ref, s0_ref, lvl_ref, o_ref, sout_ref, s_ref, *, c):
    ci = pl.program_id(1)
    n_lev = int(math.log2(c))
    pair_w = 2 * GLA_DK

    @pl.when(ci == 0)
    def _():
        s_ref[...] = s0_ref[...]

    la = la_ref[...]
    la_hi, la_lo = _split2(la)
    lvl = lvl_ref[...]
    x_all = _dot(lvl, la_hi) + _dot(lvl, la_lo)
    b = x_all[0:c]
    b_last = b[c - 1:c, :]
    q = q_ref[...]
    k = k_ref[...]
    qe = q * jnp.exp(b)
    kd = (k * jnp.exp(b_last - b)).astype(BF16)

    ti = lax.broadcasted_iota(jnp.int32, (c, c), 0)
    si = lax.broadcasted_iota(jnp.int32, (c, c), 1)
    trow = lax.broadcasted_iota(jnp.int32, (c, 1), 0)
    lane = lax.broadcasted_iota(jnp.int32, (1, pair_w), 1)
    ones_cw = jnp.ones((c, pair_w), BF16)
    srow = lax.broadcasted_iota(jnp.int32, (pair_w, 1), 0)

    for p in range(GLA_HEADS // 2):
        cols = slice(p * pair_w, (p + 1) * pair_w)
        q_p = q[:, cols]
        k_p = k[:, cols]
        qe_p = qe[:, cols]
        head_lanes = (lane < GLA_DK, lane >= GLA_DK)
        s_pair = s_ref[cols, :]
        s_pair_b = s_pair.astype(BF16)

        a = [jnp.where(ti == si, _dot_nt(jnp.where(hm, q_p, 0.0).astype(BF16), k_p.astype(BF16)), 0.0)
             for hm in head_lanes]
        for lev in range(n_lev):
            m_log = n_lev - 1 - lev
            e = jnp.exp(x_all[(lev + 1) * c:(lev + 2) * c, cols])
            upper = ((trow >> m_log) & 1) == 1
            kt = jnp.where(upper, 0.0, k_p * e).astype(BF16)
            qt = jnp.where(upper, q_p * e, 0.0)
            same = (ti >> (m_log + 1)) == (si >> (m_log + 1))
            for hh in range(2):
                al = _dot_nt(jnp.where(head_lanes[hh], qt, 0.0).astype(BF16), kt)
                a[hh] = a[hh] + jnp.where(same, al, 0.0)

        u = []
        for hh in range(2):
            h = 2 * p + hh
            v_h = v_ref[:, h * GLA_DV:(h + 1) * GLA_DV].astype(BF16)
            o_h = _dot(a[hh].astype(BF16), v_h) + _dot(jnp.where(head_lanes[hh], qe_p, 0.0).astype(BF16), s_pair_b)
            o_ref[:, h * GLA_DV:(h + 1) * GLA_DV] = o_h
            u.append(_dot_tn(kd[:, cols], v_h))
        bl_rows = _dot_tn(la_hi[:, cols], ones_cw) + _dot_tn(la_lo[:, cols], ones_cw)
        s_ref[cols, :] = jnp.exp(bl_rows[:, :GLA_DV]) * s_pair + jnp.where(srow < GLA_DK, u[0], u[1])

    @pl.when(ci == pl.num_programs(1) - 1)
    def _():
        sout_ref[...] = s_ref[...]


def _gla(q, k, v, la, s0, c):
    bsz, l, _ = q.shape
    assert l % c == 0 and (c & (c - 1)) == 0 and GLA_DV == 2 * GLA_DK
    lvl = _gla_level_matrix(c)
    tok = lambda b, i: (b, i, 0)
    per_b = lambda b, i: (b, 0, 0)
    return pl.pallas_call(
        functools.partial(_gla_kernel, c=c),
        grid=(bsz, l // c),
        in_specs=[
            pl.BlockSpec((None, c, GLA_QK_W), tok),
            pl.BlockSpec((None, c, GLA_QK_W), tok),
            pl.BlockSpec((None, c, GLA_V_W), tok),
            pl.BlockSpec((None, c, GLA_QK_W), tok),
            pl.BlockSpec((None, GLA_QK_W, GLA_DV), per_b),
            pl.BlockSpec(lvl.shape, lambda b, i: (0, 0)),
        ],
        out_specs=[pl.BlockSpec((None, c, GLA_V_W), tok), pl.BlockSpec((None, GLA_QK_W, GLA_DV), per_b)],
        out_shape=[jax.ShapeDtypeStruct((bsz, l, GLA_V_W), F32), jax.ShapeDtypeStruct((bsz, GLA_QK_W, GLA_DV), F32)],
        scratch_shapes=[pltpu.VMEM((GLA_QK_W, GLA_DV), F32)],
        compiler_params=_params(2),
        name="gla",
    )(q, k, v, la, s0, lvl)


def _select_topk_blocks(gate, n_valid):
    row = lax.broadcasted_iota(jnp.int32, gate.shape, 0)
    g = jnp.where(row < n_valid, gate, NEG)
    sel = jnp.zeros(gate.shape, F32)
    for r in range(MOBA_TOPK):
        mx = jnp.max(g, axis=0, keepdims=True)
        idx = jnp.min(jnp.where(g == mx, row, gate.shape[0]), axis=0, keepdims=True)
        pick = row == jnp.where(r < n_valid, idx, -1)
        sel = jnp.where(pick, 1.0, sel)
        g = jnp.where(pick, NEG, g)
    return sel


def _moba_prompt_kernel(q_ref, k_ref, v_ref, o_ref, kmean_ref, sel_ref, *, nb):
    qi = pl.program_id(2)
    blk = MOBA_BLOCK
    hd = MOBA_HD
    heads = range(2)

    def blk_ds(n):
        return pl.ds(pl.multiple_of(n * blk, blk), blk)

    @pl.when(qi == 0)
    def _():
        kmean_ref[...] = jnp.zeros(kmean_ref.shape, F32)

        def body(n, carry):
            kb = k_ref[blk_ds(n), :].astype(F32)
            kmean_ref[pl.ds(n, 1), :] = jnp.sum(kb, axis=0, keepdims=True) * (1.0 / blk)
            return carry

        lax.fori_loop(0, nb, body, 0)

    row = lax.broadcasted_iota(jnp.int32, (2 * hd, blk), 0)
    q_both = q_ref[...]
    q_pair = [jnp.where((row // hd) == hh, q_both, jnp.zeros((), BF16)) for hh in heads]
    km_hi, km_lo = _split2(kmean_ref[...])
    for hh in heads:
        gate = _dot(km_hi, q_pair[hh]) + _dot(km_lo, q_pair[hh])
        sel_ref[hh] = _select_topk_blocks(gate, qi)

    ones_rows = jnp.where(lax.broadcasted_iota(jnp.int32, (16, blk), 0) == 0, 1.0, 0.0).astype(BF16)

    def v_aug(hh, n):
        return jnp.concatenate([v_ref[hh * hd:(hh + 1) * hd, blk_ds(n)].astype(BF16), ones_rows], axis=0)

    def update(carry, scores, v_tiles, picked):
        m, acc = carry
        m_new = m
        for s, pk in zip(scores, picked):
            bm = jnp.max(s, axis=0, keepdims=True)
            m_new = jnp.maximum(m_new, bm if pk is None else jnp.where(pk, bm, NEG))
        acc = acc * jnp.exp2(m - m_new)
        for s, v_t, pk in zip(scores, v_tiles, picked):
            c = _dot(v_t, jnp.exp2(s - m_new).astype(BF16))
            acc = acc + (c if pk is None else jnp.where(pk, c, 0.0))
        return m_new, acc

    def past_group(gi, carries):
        out = []
        for hh in heads:
            scores, v_tiles, picked = [], [], []
            for g in range(MOBA_GROUP):
                n = jnp.minimum(gi * MOBA_GROUP + g, nb - 1)
                scores.append(_dot(k_ref[blk_ds(n), :], q_pair[hh]))
                picked.append(sel_ref[hh, pl.ds(n, 1), :] > 0.5)
                v_tiles.append(v_aug(hh, n))
            out.append(update(carries[hh], scores, v_tiles, picked))
        return tuple(out)

    causal = lax.broadcasted_iota(jnp.int32, (blk, blk), 0) <= lax.broadcasted_iota(jnp.int32, (blk, blk), 1)
    k_own = k_ref[blk_ds(qi), :]
    init = (jnp.full((1, blk), NEG, F32), jnp.zeros((hd + 16, blk), F32))
    carries = tuple(update(init, [jnp.where(causal, _dot(k_own, q_pair[hh]), NEG)], [v_aug(hh, qi)], [None])
                    for hh in heads)
    carries = lax.fori_loop(0, (qi + MOBA_GROUP - 1) // MOBA_GROUP, past_group, carries)
    o_ref[...] = jnp.concatenate([jnp.transpose(acc[:hd] / acc[hd:hd + 1]) for _, acc in carries], axis=1)


def _moba_prompt(qt, k_nat, vt):
    bsz, _, l = qt.shape
    blk = MOBA_BLOCK
    pair_w = 2 * MOBA_HD
    assert l % blk == 0 and pair_w == LANES
    nb = l // blk
    nb_pad = -(-nb // SUBLANES) * SUBLANES
    return pl.pallas_call(
        functools.partial(_moba_prompt_kernel, nb=nb),
        grid=(bsz, MOBA_HEADS // 2, nb),
        in_specs=[
            pl.BlockSpec((None, pair_w, blk), lambda b, p, i: (b, p, i)),
            pl.BlockSpec((None, l, pair_w), lambda b, p, i: (b, 0, p)),
            pl.BlockSpec((None, pair_w, l), lambda b, p, i: (b, p, 0)),
        ],
        out_specs=pl.BlockSpec((None, blk, pair_w), lambda b, p, i: (b, i, p)),
        out_shape=jax.ShapeDtypeStruct((bsz, l, MOBA_W), F32),
        scratch_shapes=[pltpu.VMEM((nb_pad, pair_w), F32), pltpu.VMEM((2, nb_pad, blk), F32)],
        compiler_params=_params(3),
        name="moba_prompt",
    )(qt, k_nat, vt)


def _select_topk_lanes(gate, n_valid):
    lane = lax.broadcasted_iota(jnp.int32, gate.shape, 1)
    g = jnp.where(lane < n_valid, gate, NEG)
    sel = jnp.zeros(gate.shape, F32)
    for r in range(min(MOBA_TOPK, n_valid)):
        mx = jnp.max(g, axis=1, keepdims=True)
        idx = jnp.min(jnp.where(g == mx, lane, gate.shape[1]), axis=1, keepdims=True)
        pick = lane == idx
        sel = jnp.where(pick, 1.0, sel)
        g = jnp.where(pick, NEG, g)
    return sel


def _moba_sample_kernel(pt_ref, qb_ref, kn_ref, vn_ref, *refs, npp, n_steps, n_pages, page, t_new):
    k_refs = refs[:npp]
    v_refs = refs[npp:2 * npp]
    o_ref = refs[2 * npp]
    s_ref, p_ref, acc_ref = refs[2 * npp + 1:]
    j = pl.program_id(1)
    blk = MOBA_BLOCK
    n_blk = (n_pages * page) // blk
    past = n_pages * page
    n_c = qb_ref.shape[0]
    t_pad = n_c // MOBA_HEADS

    def flat(ref):
        return ref[...].reshape(MOBA_W, page)

    @pl.when(j < n_steps)
    def _():
        qb = qb_ref[...]
        for i in range(0, npp, 2):
            w = jnp.concatenate([flat(k_refs[i]), flat(k_refs[i + 1])], axis=1).astype(BF16)
            cols = pl.ds(pl.multiple_of((j * npp + i) * page, 2 * page), 2 * page)
            s_ref[:, cols] = _dot(qb, w)

    @pl.when(j == n_steps - 1)
    def _():
        qb = qb_ref[...]
        lane = lax.broadcasted_iota(jnp.int32, (n_c, LANES), 1)

        def blk_cols(n):
            return pl.ds(pl.multiple_of(n * blk, blk), blk)

        def gate_body(n, gate):
            gsum = jnp.sum(s_ref[:, blk_cols(n)], axis=1, keepdims=True) * (1.0 / blk)
            return jnp.where(lane == n, gsum, gate)

        unroll = math.gcd(n_blk, 8)
        gate = lax.fori_loop(0, n_blk, gate_body, jnp.zeros((n_c, LANES), F32), unroll=unroll)
        sel = _select_topk_lanes(gate, n_blk)

        s_own = _dot(qb, kn_ref[...].astype(BF16))
        key_t = lax.broadcasted_iota(jnp.int32, (n_c, page), 1)
        row_t = lax.broadcasted_iota(jnp.int32, (n_c, page), 0) % t_pad
        s_own = jnp.where((key_t < t_new) & (key_t <= row_t), s_own, NEG)

        def masked(n):
            picked = jnp.sum(jnp.where(lane == n, sel, 0.0), axis=1, keepdims=True) > 0.5
            return jnp.where(picked, s_ref[:, blk_cols(n)], NEG)

        def max_body(n, m):
            return jnp.maximum(m, masked(n))

        m_run = lax.fori_loop(0, n_blk, max_body, jnp.full((n_c, blk), NEG, F32), unroll=unroll)
        m = jnp.maximum(jnp.max(m_run, axis=1, keepdims=True), jnp.max(s_own, axis=1, keepdims=True))

        def exp_body(n, l_run):
            e = jnp.exp(masked(n) - m)
            s_ref[:, blk_cols(n)] = e
            return l_run + e

        l_run = lax.fori_loop(0, n_blk, exp_body, jnp.zeros((n_c, blk), F32), unroll=unroll)
        e_own = jnp.exp(s_own - m)
        inv = 1.0 / (jnp.sum(l_run, axis=1, keepdims=True) + jnp.sum(e_own, axis=1, keepdims=True))

        def norm_body(n, carry):
            p_ref[:, blk_cols(n)] = (s_ref[:, blk_cols(n)] * inv).astype(BF16)
            return carry

        lax.fori_loop(0, n_blk, norm_body, 0, unroll=unroll)
        p_ref[:, past:past + page] = (e_own * inv).astype(BF16)
        acc_ref[...] = jnp.zeros(acc_ref.shape, F32)

    @pl.when(j >= n_steps)
    def _():
        v_cat = jnp.concatenate([flat(v_refs[i]) for i in range(npp)], axis=1).astype(BF16)
        cols = pl.ds(pl.multiple_of((j - n_steps) * (npp * page), npp * page), npp * page)
        acc_ref[...] += _dot_nt(v_cat, p_ref[:, cols])

    @pl.when(j == 2 * n_steps - 1)
    def _():
        o_ref[...] = acc_ref[...] + _dot_nt(vn_ref[...].astype(BF16), p_ref[:, past:past + page])


def _moba_sample(mq, mk, mv, cache_kt, cache_vt, layer, page_table):
    dbs, t_new, _ = mq.shape
    page = cache_kt.shape[-1]
    n_pages = page_table.shape[1]
    npp = SAMPLE_PAGES_PER_STEP
    t_pad = t_new + t_new % 2
    n_c = MOBA_HEADS * t_pad
    assert n_pages % npp == 0 and npp % 2 == 0 and page == LANES
    assert (n_pages * page) % MOBA_BLOCK == 0 and (n_pages * page) // MOBA_BLOCK <= LANES and t_new <= page
    n_steps = n_pages // npp

    q4 = (mq * (MOBA_HD ** -0.5)).reshape(dbs, t_new, MOBA_HEADS, MOBA_HD)
    q4 = jnp.pad(q4, ((0, 0), (0, t_pad - t_new), (0, 0), (0, 0)))
    eye = jnp.eye(MOBA_HEADS, dtype=F32)
    qb = (q4[:, :, :, None, :] * eye[None, None, :, :, None]).transpose(0, 2, 1, 3, 4)
    qb = qb.reshape(dbs, n_c, MOBA_W).astype(BF16)
    kn = jnp.pad(jnp.transpose(mk, (0, 2, 1)), ((0, 0), (0, 0), (0, page - t_new)))
    vn = jnp.pad(jnp.transpose(mv, (0, 2, 1)), ((0, 0), (0, 0), (0, page - t_new)))
    pt_flat = page_table.reshape(-1).astype(jnp.int32)

    def k_map(i):
        return lambda b, j, pt: (layer, pt[b * n_pages + jnp.minimum(j, n_steps - 1) * npp + i], 0, 0, 0)

    def v_map(i):
        return lambda b, j, pt: (layer, pt[b * n_pages + jnp.maximum(j - n_steps, 0) * npp + i], 0, 0, 0)

    per_b = lambda b, j, pt: (b, 0, 0)
    page_block = (None, None, MOBA_HEADS, MOBA_HD, page)
    grid_spec = pltpu.PrefetchScalarGridSpec(
        num_scalar_prefetch=1,
        grid=(dbs, 2 * n_steps),
        in_specs=[pl.BlockSpec((None, n_c, MOBA_W), per_b),
                  pl.BlockSpec((None, MOBA_W, page), per_b),
                  pl.BlockSpec((None, MOBA_W, page), per_b)]
                 + [pl.BlockSpec(page_block, k_map(i)) for i in range(npp)]
                 + [pl.BlockSpec(page_block, v_map(i)) for i in range(npp)],
        out_specs=pl.BlockSpec((None, MOBA_W, n_c), per_b),
        scratch_shapes=[
            pltpu.VMEM((n_c, n_pages * page), F32),
            pltpu.VMEM((n_c, (n_pages + 1) * page), BF16),
            pltpu.VMEM((MOBA_W, n_c), F32),
        ],
    )
    out_t = pl.pallas_call(
        functools.partial(_moba_sample_kernel, npp=npp, n_steps=n_steps, n_pages=n_pages, page=page, t_new=t_new),
        grid_spec=grid_spec,
        out_shape=jax.ShapeDtypeStruct((dbs, MOBA_W, n_c), F32),
        compiler_params=_params(2),
        name="moba_sample",
    )(pt_flat, qb, kn, vn, *([cache_kt] * npp), *([cache_vt] * npp))
    o5 = out_t.reshape(dbs, MOBA_HEADS, MOBA_HD, MOBA_HEADS, t_pad)
    o = jnp.sum(o5 * eye[None, :, None, :, None], axis=3)
    return o.transpose(0, 3, 1, 2)[:, :t_new].reshape(dbs, t_new, MOBA_W)


def _outproj_kernel(x_ref, go_ref, gr_ref, mo_ref, ggo_ref, wo1_ref, wo2_ref, gffn_ref, wr_ref, br_ref,
                    h_ref, xn_ref, eid_ref, wt_ref, rank_ref, cnt_ref, carry_ref):
    i = pl.program_id(0)
    t = x_ref.shape[0]

    @pl.when(i == 0)
    def _():
        carry_ref[...] = jnp.zeros(carry_ref.shape, F32)

    gr = gr_ref[...]
    gate = gr / (1.0 + jnp.exp(-gr))
    parts = []
    for hd in range(GLA_HEADS):
        cols = slice(hd * GLA_DV, (hd + 1) * GLA_DV)
        o = go_ref[:, cols]
        ms = jnp.mean(o * o, axis=-1, keepdims=True)
        parts.append(o * lax.rsqrt(ms + EPS) * ggo_ref[...] * gate[:, cols])
    a1 = jnp.concatenate(parts, axis=1).astype(BF16)
    h = x_ref[...] + _dot(a1, wo1_ref[...]) + _dot(mo_ref[...].astype(BF16), wo2_ref[...])
    h_ref[...] = h

    ms = jnp.mean(h * h, axis=-1, keepdims=True)
    xn = h * lax.rsqrt(ms + EPS) * gffn_ref[...]
    xn_ref[...] = xn
    logits = _dot_nt(wr_ref[...], xn.astype(BF16)) + br_ref[...]

    gl = logits[0:N_GROUPS]
    grow = lax.broadcasted_iota(jnp.int32, gl.shape, 0)
    gmax = jnp.max(gl, axis=0, keepdims=True)
    g_sel = jnp.min(jnp.where(gl == gmax, grow, N_GROUPS), axis=0, keepdims=True)
    gw = 1.0 / jnp.sum(jnp.exp(gl - gmax), axis=0, keepdims=True)

    el = logits[SUBLANES:SUBLANES + N_EXPERTS]
    erow = lax.broadcasted_iota(jnp.int32, el.shape, 0)
    in_grp = (erow // EXPERTS_PER_GROUP) == g_sel
    emax = jnp.max(jnp.where(in_grp, el, NEG), axis=0, keepdims=True)
    ee = jnp.where(in_grp, jnp.exp(el - emax), 0.0)
    ep = ee / jnp.sum(ee, axis=0, keepdims=True)
    cand = jnp.where(in_grp, ep, -1.0)
    p1 = jnp.max(cand, axis=0, keepdims=True)
    i1 = jnp.min(jnp.where(cand == p1, erow, N_EXPERTS), axis=0, keepdims=True)
    cand = jnp.where(erow == i1, -1.0, cand)
    p2 = jnp.max(cand, axis=0, keepdims=True)
    i2 = jnp.min(jnp.where(cand == p2, erow, N_EXPERTS), axis=0, keepdims=True)
    denom = p1 + p2
    zero_i = jnp.zeros((SUBLANES - EXPERT_TOPK, t), jnp.int32)
    zero_f = jnp.zeros((SUBLANES - EXPERT_TOPK, t), F32)
    eid_ref[...] = jnp.concatenate([i1, i2, zero_i], axis=0)
    wt_ref[...] = jnp.concatenate([p1 / denom * gw, p2 / denom * gw, zero_f], axis=0)

    oh1 = erow == i1
    oh2 = erow == i2
    hits = jnp.where(oh1 | oh2, 1.0, 0.0)
    before = (lax.broadcasted_iota(jnp.int32, (t, t), 0) < lax.broadcasted_iota(jnp.int32, (t, t), 1))
    prefix = _dot(hits.astype(BF16), jnp.where(before, 1.0, 0.0).astype(BF16)) + carry_ref[...]
    r1 = jnp.sum(jnp.where(oh1, prefix, 0.0), axis=0, keepdims=True)
    r2 = jnp.sum(jnp.where(oh2, prefix, 0.0), axis=0, keepdims=True)
    rank_ref[...] = jnp.concatenate([r1.astype(jnp.int32), r2.astype(jnp.int32), zero_i], axis=0)
    carry_ref[...] += jnp.sum(hits, axis=1, keepdims=True)
    cnt_ref[...] = jnp.broadcast_to(carry_ref[...], cnt_ref.shape)


def _outproj(x2d, go, gr, mo, w):
    n, d = x2d.shape
    t = ROW_TILE
    assert n % t == 0
    row = lambda i: (i, 0)
    col = lambda i: (0, i)
    fixed = lambda i: (0, 0)
    return pl.pallas_call(
        _outproj_kernel,
        grid=(n // t,),
        in_specs=[
            pl.BlockSpec((t, d), row),
            pl.BlockSpec((t, GLA_V_W), row),
            pl.BlockSpec((t, GLA_V_W), row),
            pl.BlockSpec((t, MOBA_W), row),
            pl.BlockSpec((1, GLA_DV), fixed),
            pl.BlockSpec(w["wo1"].shape, fixed),
            pl.BlockSpec(w["wo2"].shape, fixed),
            pl.BlockSpec((1, d), fixed),
            pl.BlockSpec(w["wr"].shape, fixed),
            pl.BlockSpec(w["br"].shape, fixed),
        ],
        out_specs=[pl.BlockSpec((t, d), row), pl.BlockSpec((t, d), row),
                   pl.BlockSpec((SUBLANES, t), col), pl.BlockSpec((SUBLANES, t), col),
                   pl.BlockSpec((SUBLANES, t), col), pl.BlockSpec((N_EXPERTS, LANES), fixed)],
        out_shape=[jax.ShapeDtypeStruct((n, d), F32), jax.ShapeDtypeStruct((n, d), F32),
                   jax.ShapeDtypeStruct((SUBLANES, n), jnp.int32), jax.ShapeDtypeStruct((SUBLANES, n), F32),
                   jax.ShapeDtypeStruct((SUBLANES, n), jnp.int32), jax.ShapeDtypeStruct((N_EXPERTS, LANES), F32)],
        scratch_shapes=[pltpu.VMEM((N_EXPERTS, 1), F32)],
        compiler_params=_params(1),
        name="outproj",
    )(x2d, go, gr, mo, w["g_gla_out"], w["wo1"], w["wo2"], w["g_ffn"], w["wr"], w["br"])


def _row_copy(src_ref, src_row, dst_ref, dst_row, sem):
    return pltpu.make_async_copy(src_ref.at[pl.ds(src_row, 1)], dst_ref.at[pl.ds(dst_row, 1)], sem)


def _moe_scatter_kernel(dest_ref, x_ref, xs_in_ref, xs_ref, sem):
    del xs_in_ref
    t = x_ref.shape[0]

    def start(r, carry):
        for kk in range(EXPERT_TOPK):
            _row_copy(x_ref, r, xs_ref, dest_ref[kk, r], sem).start(priority=kk % 2)
        return carry

    def wait(r, carry):
        for kk in range(EXPERT_TOPK):
            _row_copy(x_ref, 0, xs_ref, 0, sem).wait()
        return carry

    lax.fori_loop(0, t, start, 0, unroll=ROW_DMA_UNROLL)
    lax.fori_loop(0, t, wait, 0, unroll=ROW_DMA_UNROLL)


def _moe_scatter(xn, dest, n_rows):
    n, d = xn.shape
    t = ROW_TILE
    xs0 = jnp.zeros((n_rows, d), F32)
    return pl.pallas_call(
        _moe_scatter_kernel,
        grid=(n // t,),
        in_specs=[
            pl.BlockSpec((EXPERT_TOPK, t), lambda i: (0, i), memory_space=pltpu.SMEM),
            pl.BlockSpec((t, d), lambda i: (i, 0)),
            pl.BlockSpec(memory_space=pl.ANY),
        ],
        out_specs=pl.BlockSpec(memory_space=pl.ANY),
        out_shape=jax.ShapeDtypeStruct((n_rows, d), F32),
        scratch_shapes=[pltpu.SemaphoreType.DMA(())],
        input_output_aliases={2: 0},
        compiler_params=_params(1),
        name="moe_scatter",
    )(dest, xn, xs0)


def _moe_ffn_kernel(be_ref, nu_ref, xs_ref, wg_ref, wu_ref, wd_ref, ys_ref):
    i = pl.program_id(0)

    @pl.when(i < nu_ref[0])
    def _():
        x = xs_ref[...].astype(BF16)
        g = _dot(x, wg_ref[...])
        u = _dot(x, wu_ref[...])
        hmid = (g / (1.0 + jnp.exp(-g)) * u).astype(BF16)
        ys_ref[...] = _dot(hmid, wd_ref[...])

    @pl.when(i >= nu_ref[0])
    def _():
        ys_ref[...] = jnp.zeros(ys_ref.shape, F32)


def _moe_ffn(xs, blk_expert, n_used, wg, wu, wd):
    r, d = xs.shape
    de = wg.shape[2]
    n_blk = r // MOE_BLOCK
    rows = lambda i, be, nu: (jnp.minimum(i, nu[0] - 1), 0)
    wsel = lambda i, be, nu: (be[jnp.minimum(i, nu[0] - 1)], 0, 0)
    grid_spec = pltpu.PrefetchScalarGridSpec(
        num_scalar_prefetch=2,
        grid=(n_blk,),
        in_specs=[
            pl.BlockSpec((MOE_BLOCK, d), rows),
            pl.BlockSpec((None, d, de), wsel),
            pl.BlockSpec((None, d, de), wsel),
            pl.BlockSpec((None, de, d), wsel),
        ],
        out_specs=pl.BlockSpec((MOE_BLOCK, d), lambda i, be, nu: (i, 0)),
    )
    return pl.pallas_call(
        _moe_ffn_kernel,
        grid_spec=grid_spec,
        out_shape=jax.ShapeDtypeStruct((r, d), F32),
        compiler_params=_params(1),
        name="moe_ffn",
    )(blk_expert, n_used, xs, wg, wu, wd)


def _moe_combine_kernel(dest_ref, h_ref, wt_ref, ys_ref, y_ref, buf_ref, sem):
    t = h_ref.shape[0]

    def start(r, carry):
        for kk in range(EXPERT_TOPK):
            _row_copy(ys_ref, dest_ref[kk, r], buf_ref.at[kk], r, sem).start(priority=kk % 2)
        return carry

    def wait(r, carry):
        for kk in range(EXPERT_TOPK):
            _row_copy(ys_ref, 0, buf_ref.at[kk], 0, sem).wait()
        return carry

    lax.fori_loop(0, t, start, 0, unroll=ROW_DMA_UNROLL)
    lax.fori_loop(0, t, wait, 0, unroll=ROW_DMA_UNROLL)
    y = h_ref[...]
    for kk in range(EXPERT_TOPK):
        y = y + buf_ref[kk] * wt_ref[:, kk:kk + 1]
    y_ref[...] = y


def _moe_combine(h, wt_rows, dest, ys):
    n, d = h.shape
    t = ROW_TILE
    return pl.pallas_call(
        _moe_combine_kernel,
        grid=(n // t,),
        in_specs=[
            pl.BlockSpec((EXPERT_TOPK, t), lambda i: (0, i), memory_space=pltpu.SMEM),
            pl.BlockSpec((t, d), lambda i: (i, 0)),
            pl.BlockSpec((t, SUBLANES), lambda i: (i, 0)),
            pl.BlockSpec(memory_space=pl.ANY),
        ],
        out_specs=pl.BlockSpec((t, d), lambda i: (i, 0)),
        out_shape=jax.ShapeDtypeStruct((n, d), F32),
        scratch_shapes=[pltpu.VMEM((EXPERT_TOPK, t, d), F32), pltpu.SemaphoreType.DMA(())],
        compiler_params=_params(1),
        name="moe_combine",
    )(dest, h, wt_rows, ys)


def _moe(h, xn, eid, wt, rank, counts, w):
    n, d = h.shape
    blk = MOE_BLOCK
    n_blk = -(-(n * EXPERT_TOPK) // blk) + N_EXPERTS
    cnt = counts[:, 0].astype(jnp.int32)
    padded = (cnt + blk - 1) // blk * blk
    pend = jnp.cumsum(padded)
    pstart = pend - padded
    dest = rank[:EXPERT_TOPK]
    for e in range(N_EXPERTS):
        dest = dest + jnp.where(eid[:EXPERT_TOPK] == e, pstart[e], 0)
    blk_expert = jnp.minimum(jnp.searchsorted(pend, jnp.arange(n_blk) * blk, side="right"),
                             N_EXPERTS - 1).astype(jnp.int32)
    n_used = (pend[-1:] // blk).astype(jnp.int32)

    xs = _moe_scatter(xn, dest, n_blk * blk)
    ys = _moe_ffn(xs, blk_expert, n_used, w["w_gate"], w["w_up"], w["w_down"])
    return _moe_combine(h, jnp.transpose(wt), dest, ys)


def _prep_weights(g_attn_norm, w_in, w_gla_gate_up, b_gla_gate, g_gla_out, g_q, g_k, w_out, g_ffn_norm,
                  w_group_router, b_group_router, w_expert_router, b_expert_router, w_gate, w_up, w_down):
    d = w_in.shape[0]
    o = np.cumsum((GLA_QK_W, GLA_QK_W, GLA_V_W, GLA_V_W, GLA_GATE_RANK, MOBA_W, MOBA_W, MOBA_W))
    wlr = jnp.pad(w_in[:, o[3]:o[4]], ((0, 0), (0, LANES - GLA_GATE_RANK)))
    wup = jnp.pad(w_gla_gate_up, ((0, LANES - GLA_GATE_RANK), (0, 0)))
    wup_hi = wup.astype(BF16)
    wup_lo = (wup - wup_hi.astype(F32)).astype(BF16)
    wr = jnp.zeros((4 * SUBLANES, d), F32)
    wr = wr.at[0:N_GROUPS].set(w_group_router.T).at[SUBLANES:SUBLANES + N_EXPERTS].set(w_expert_router.T)
    br = jnp.zeros((4 * SUBLANES, 1), F32)
    br = br.at[0:N_GROUPS, 0].set(b_group_router).at[SUBLANES:SUBLANES + N_EXPERTS, 0].set(b_expert_router)
    return {
        "g_attn": g_attn_norm[None, :],
        "wg": w_in[:, :o[3]].astype(BF16),
        "wlr": wlr.astype(BF16),
        "wup": jnp.stack([wup_hi, wup_lo]),
        "b_gate": b_gla_gate[None, :],
        "wm": w_in[:, o[4]:].astype(BF16),
        "g_q": jnp.tile(g_q, MOBA_HEADS)[None, :],
        "g_k": jnp.tile(g_k, MOBA_HEADS)[None, :],
        "g_gla_out": g_gla_out[None, :],
        "wo1": w_out[:GLA_V_W].astype(BF16),
        "wo2": w_out[GLA_V_W:].astype(BF16),
        "g_ffn": g_ffn_norm[None, :],
        "wr": wr.astype(BF16),
        "br": br,
        "w_gate": w_gate.astype(BF16),
        "w_up": w_up.astype(BF16),
        "w_down": w_down.astype(BF16),
    }


def _layer(x, positions, s0, gla_chunk, sample_moba, w):
    bsz, l, d = x.shape
    n = bsz * l
    x2d = x.reshape(n, d)
    cos_tab, sin_tab = _rope_tables(positions)
    if sample_moba is None:
        gq, gk, gv, gr, la, qt, kt, vt, k_nat = _inproj(x2d, w, cos_tab, sin_tab, seq_len=l)
        mo = _moba_prompt(qt, k_nat, vt).reshape(n, MOBA_W)
        k_new = kt.reshape(bsz, MOBA_HEADS, MOBA_HD, l).transpose(0, 3, 1, 2)
        v_new = vt.reshape(bsz, MOBA_HEADS, MOBA_HD, l).transpose(0, 3, 1, 2)
    else:
        gq, gk, gv, gr, la, mq, mk, mv = _inproj(x2d, w, cos_tab, sin_tab)
        mk3 = mk.reshape(bsz, l, MOBA_W)
        mv3 = mv.reshape(bsz, l, MOBA_W)
        mo = sample_moba(mq.reshape(bsz, l, MOBA_W), mk3, mv3).reshape(n, MOBA_W)
        k_new = mk3.reshape(bsz, l, MOBA_HEADS, MOBA_HD)
        v_new = mv3.reshape(bsz, l, MOBA_HEADS, MOBA_HD)

    lp = -(-l // gla_chunk) * gla_chunk
    seq = lambda a: jnp.pad(a.reshape(bsz, l, -1), ((0, 0), (0, lp - l), (0, 0)))
    go, s_new = _gla(seq(gq), seq(gk), seq(gv), seq(la), s0.reshape(bsz, GLA_QK_W, GLA_DV), gla_chunk)
    go = go[:, :l].reshape(n, GLA_V_W)

    h, xn, eid, wt, rank, counts = _outproj(x2d, go, gr, mo, w)
    y = _moe(h, xn, eid, wt, rank, counts, w)
    return y.reshape(bsz, l, d), k_new, v_new, s_new.reshape(bsz, GLA_HEADS, GLA_DK, GLA_DV)


def kernel(x_prompt, x_sample, cache_k, cache_v, state_gla, page_table, g_attn_norm, w_in, w_gla_gate_up, b_gla_gate,
           g_gla_out, g_q, g_k, w_out, g_ffn_norm, w_group_router, b_group_router, w_expert_router, b_expert_router,
           w_gate, w_up, w_down):
    depth = w_in.shape[0]
    bsz, l, _ = x_prompt.shape
    dbs, t_new, _ = x_sample.shape
    n_phys, page = cache_k.shape[1], cache_k.shape[2]
    past = page_table.shape[1] * page
    assert l % ROW_TILE == 0 and (dbs * t_new) % ROW_TILE == 0 and ROW_TILE % t_new == 0

    pos_p = np.arange(l)
    pos_s = past + (np.arange(ROW_TILE) % t_new)
    chunk_s = max(SUBLANES, 1 << (t_new - 1).bit_length())
    cache_kt = jnp.transpose(cache_k, (0, 1, 3, 4, 2))
    cache_vt = jnp.transpose(cache_v, (0, 1, 3, 4, 2))

    hp, hs = x_prompt, x_sample
    outs = [[] for _ in range(6)]
    for li in range(depth):
        w = _prep_weights(g_attn_norm[li], w_in[li], w_gla_gate_up[li], b_gla_gate[li], g_gla_out[li], g_q[li],
                          g_k[li], w_out[li], g_ffn_norm[li], w_group_router[li], b_group_router[li],
                          w_expert_router[li], b_expert_router[li], w_gate[li], w_up[li], w_down[li])
        s0 = jnp.zeros((bsz, GLA_HEADS, GLA_DK, GLA_DV), F32)
        hp, kp, vp, sp = _layer(hp, pos_p, s0, min(GLA_CHUNK, l), None, w)

        moba_s = lambda q, k, v, li=li: _moba_sample(q, k, v, cache_kt, cache_vt, li, page_table)
        hs, ks, vs, ss = _layer(hs, pos_s, state_gla[li].astype(F32), chunk_s, moba_s, w)
        for lst, val in zip(outs, (kp, vp, sp.astype(state_gla.dtype), ks, vs, ss.astype(state_gla.dtype))):
            lst.append(val)

    return (hp, hs) + tuple(jnp.stack(o) for o in outs)
```

```python
import functools
import math

import jax
import jax.numpy as jnp
import numpy as np
from jax import lax
from jax.experimental import pallas as pl
from jax.experimental.pallas import tpu as pltpu

GLA_HEADS = 4
GLA_DK = 64
GLA_DV = 128
GLA_GATE_RANK = 16
GLA_TAU = 16.0
MOBA_HEADS = 8
MOBA_HD = 64
MOBA_BLOCK = 256
MOBA_TOPK = 3
ROPE_THETA = 10000.0
N_GROUPS = 4
EXPERTS_PER_GROUP = 4
N_EXPERTS = N_GROUPS * EXPERTS_PER_GROUP
EXPERT_TOPK = 2
EPS = 1e-6

GLA_QK_W = GLA_HEADS * GLA_DK
GLA_V_W = GLA_HEADS * GLA_DV
MOBA_W = MOBA_HEADS * MOBA_HD

LANES = 128
SUBLANES = 8
VMEM_LIMIT_BYTES = 56 * 1024 * 1024

ROW_TILE = 256
GLA_CHUNK = 256
MOE_BLOCK = 256
SAMPLE_PAGES_PER_STEP = 8
ROW_DMA_UNROLL = 8
MOBA_GROUP = 4
F32 = jnp.float32
BF16 = jnp.bfloat16
NEG = -1e30
LOG2E = math.log2(math.e)


def _dot(a, b):
    return jnp.dot(a, b, preferred_element_type=F32)


def _dot_nt(a, b):
    return lax.dot_general(a, b, (((1,), (1,)), ((), ())), preferred_element_type=F32)


def _dot_tn(a, b):
    return lax.dot_general(a, b, (((0,), (0,)), ((), ())), preferred_element_type=F32)


def _split2(x):
    hi = x.astype(BF16)
    lo = (x - hi.astype(F32)).astype(BF16)
    return hi, lo


def _params(n_axes):
    return pltpu.CompilerParams(dimension_semantics=("arbitrary",) * n_axes,
                                vmem_limit_bytes=VMEM_LIMIT_BYTES)


def _head_norm(y, g):
    lo = lax.broadcasted_iota(jnp.int32, (1, LANES), 1) < MOBA_HD
    outs = []
    for p in range(MOBA_W // LANES):
        yp = y[:, p * LANES:(p + 1) * LANES]
        sq = yp * yp
        s_lo = jnp.sum(jnp.where(lo, sq, 0.0), axis=-1, keepdims=True)
        s_hi = jnp.sum(jnp.where(lo, 0.0, sq), axis=-1, keepdims=True)
        ms = jnp.where(lo, s_lo, s_hi) * (1.0 / MOBA_HD)
        outs.append(yp * lax.rsqrt(ms + EPS))
    return jnp.concatenate(outs, axis=1) * g


def _rope(y, cos, sin_signed):
    n = y.shape[1]
    first = (lax.broadcasted_iota(jnp.int32, (1, n), 1) % MOBA_HD) < (MOBA_HD // 2)
    rot = jnp.where(first, pltpu.roll(y, n - MOBA_HD // 2, 1), pltpu.roll(y, MOBA_HD // 2, 1))
    return y * cos + rot * sin_signed


def _inproj_kernel(x_ref, gin_ref, wg_ref, wlr_ref, wup_ref, bgate_ref, wm_ref, gq_ref, gk_ref, cos_ref, sin_ref,
                   oq_ref, ok_ref, ov_ref, or_ref, ola_ref, *moba_refs, transposed):
    x = x_ref[...]
    ms = jnp.mean(x * x, axis=-1, keepdims=True)
    xn = (x * lax.rsqrt(ms + EPS) * gin_ref[...]).astype(BF16)

    yg = _dot(xn, wg_ref[...])
    oq_ref[...] = yg[:, :GLA_QK_W] * (GLA_DK ** -0.5)
    ok_ref[...] = yg[:, GLA_QK_W:2 * GLA_QK_W]
    ov_ref[...] = yg[:, 2 * GLA_QK_W:2 * GLA_QK_W + GLA_V_W]
    or_ref[...] = yg[:, 2 * GLA_QK_W + GLA_V_W:]

    lr = _dot(xn, wlr_ref[...])
    lr_hi, lr_lo = _split2(lr)
    wu_hi = wup_ref[0]
    wu_lo = wup_ref[1]
    z = _dot(lr_hi, wu_hi) + _dot(lr_lo, wu_hi) + _dot(lr_hi, wu_lo) + bgate_ref[...]
    ola_ref[...] = (jnp.minimum(z, 0.0) - jnp.log(1.0 + jnp.exp(-jnp.abs(z)))) * (1.0 / GLA_TAU)

    ym = _dot(xn, wm_ref[...])
    reps = MOBA_W // LANES
    cos = jnp.concatenate([cos_ref[...]] * reps, axis=1)
    sin = jnp.concatenate([sin_ref[...]] * reps, axis=1)
    mq = _rope(_head_norm(ym[:, :MOBA_W], gq_ref[...]), cos, sin)
    mk = _rope(_head_norm(ym[:, MOBA_W:2 * MOBA_W], gk_ref[...]), cos, sin)
    mv = ym[:, 2 * MOBA_W:]
    if transposed:
        qt_ref, kt_ref, vt_ref, kn_ref = moba_refs
        qt_ref[...] = jnp.transpose(mq * (MOBA_HD ** -0.5 * LOG2E)).astype(BF16)
        kt_ref[...] = jnp.transpose(mk)
        vt_ref[...] = jnp.transpose(mv)
        kn_ref[...] = mk.astype(BF16)
    else:
        omq_ref, omk_ref, omv_ref = moba_refs
        omq_ref[...] = mq
        omk_ref[...] = mk
        omv_ref[...] = mv


def _inproj(x2d, w, cos_tab, sin_tab, seq_len=None):
    n, d = x2d.shape
    t = ROW_TILE
    assert n % t == 0 and cos_tab.shape[0] % t == 0
    n_pos_tiles = cos_tab.shape[0] // t
    row = lambda i: (i, 0)
    fixed = lambda i: (0, 0)
    widths = (GLA_QK_W, GLA_QK_W, GLA_V_W, GLA_V_W, GLA_QK_W)
    out_specs = [pl.BlockSpec((t, wd), row) for wd in widths]
    out_shape = [jax.ShapeDtypeStruct((n, wd), F32) for wd in widths]
    if seq_len is None:
        out_specs += [pl.BlockSpec((t, MOBA_W), row)] * 3
        out_shape += [jax.ShapeDtypeStruct((n, MOBA_W), F32)] * 3
    else:
        assert seq_len % t == 0
        tiles = seq_len // t
        bsz = n // seq_len
        tok_lanes = lambda i: (i // tiles, 0, i % tiles)
        out_specs += [pl.BlockSpec((None, MOBA_W, t), tok_lanes)] * 3
        out_specs += [pl.BlockSpec((None, t, MOBA_W), lambda i: (i // tiles, i % tiles, 0))]
        out_shape += [jax.ShapeDtypeStruct((bsz, MOBA_W, seq_len), dt) for dt in (BF16, F32, F32)]
        out_shape += [jax.ShapeDtypeStruct((bsz, seq_len, MOBA_W), BF16)]
    return pl.pallas_call(
        functools.partial(_inproj_kernel, transposed=seq_len is not None),
        grid=(n // t,),
        in_specs=[
            pl.BlockSpec((t, d), row),
            pl.BlockSpec((1, d), fixed),
            pl.BlockSpec(w["wg"].shape, fixed),
            pl.BlockSpec(w["wlr"].shape, fixed),
            pl.BlockSpec(w["wup"].shape, lambda i: (0, 0, 0)),
            pl.BlockSpec((1, GLA_QK_W), fixed),
            pl.BlockSpec(w["wm"].shape, fixed),
            pl.BlockSpec((1, MOBA_W), fixed),
            pl.BlockSpec((1, MOBA_W), fixed),
            pl.BlockSpec((t, LANES), lambda i: (i % n_pos_tiles, 0)),
            pl.BlockSpec((t, LANES), lambda i: (i % n_pos_tiles, 0)),
        ],
        out_specs=out_specs,
        out_shape=out_shape,
        compiler_params=_params(1),
        name="inproj",
    )(x2d, w["g_attn"], w["wg"], w["wlr"], w["wup"], w["b_gate"], w["wm"], w["g_q"], w["g_k"], cos_tab, sin_tab)


def _rope_tables(positions):
    half = MOBA_HD // 2
    inv = ROPE_THETA ** (-np.arange(half, dtype=np.float64) / half)
    ang = positions.astype(np.float64)[:, None] * inv[None, :]
    cos = np.cos(ang)
    sin = np.sin(ang)
    cos_h = np.concatenate([cos, cos], axis=1)
    sin_h = np.concatenate([-sin, sin], axis=1)
    reps = LANES // MOBA_HD
    return (jnp.asarray(np.tile(cos_h, (1, reps)), F32), jnp.asarray(np.tile(sin_h, (1, reps)), F32))


def _gla_level_matrix(c):
    t = np.arange(c)[:, None]
    j = np.arange(c)[None, :]
    mats = [(j <= t)]
    m = c // 2
    while m >= 1:
        base = (t // (2 * m)) * (2 * m)
        ref = base + m - 1
        upper = (t - base) >= m
        mats.append(np.where(upper, (j > ref) & (j <= t), (j > t) & (j <= ref)))
        m //= 2
    return jnp.asarray(np.concatenate(mats, axis=0).astype(np.float32), BF16)


def _gla_kernel(q_ref, k_ref, v_ref, la_ref, s0_ref, lvl_ref, o_ref, sout_ref, s_ref, *, c):
    ci = pl.program_id(1)
    n_lev = int(math.log2(c))
    pair_w = 2 * GLA_DK

    @pl.when(ci == 0)
    def _():
        s_ref[...] = s0_ref[...]

    la = la_ref[...]
    la_hi, la_lo = _split2(la)
    lvl = lvl_ref[...]
    x_all = _dot(lvl, la_hi) + _dot(lvl, la_lo)
    b = x_all[0:c]
    b_last = b[c - 1:c, :]
    q = q_ref[...]
    k = k_ref[...]
    qe = q * jnp.exp(b)
    kd = (k * jnp.exp(b_last - b)).astype(BF16)

    ti = lax.broadcasted_iota(jnp.int32, (c, c), 0)
    si = lax.broadcasted_iota(jnp.int32, (c, c), 1)
    trow = lax.broadcasted_iota(jnp.int32, (c, 1), 0)
    lane = lax.broadcasted_iota(jnp.int32, (1, pair_w), 1)
    ones_cw = jnp.ones((c, pair_w), BF16)
    srow = lax.broadcasted_iota(jnp.int32, (pair_w, 1), 0)

    for p in range(GLA_HEADS // 2):
        cols = slice(p * pair_w, (p + 1) * pair_w)
        q_p = q[:, cols]
        k_p = k[:, cols]
        qe_p = qe[:, cols]
        head_lanes = (lane < GLA_DK, lane >= GLA_DK)
        s_pair = s_ref[cols, :]
        s_pair_b = s_pair.astype(BF16)

        a = [jnp.where(ti == si, _dot_nt(jnp.where(hm, q_p, 0.0).astype(BF16), k_p.astype(BF16)), 0.0)
             for hm in head_lanes]
        for lev in range(n_lev):
            m_log = n_lev - 1 - lev
            e = jnp.exp(x_all[(lev + 1) * c:(lev + 2) * c, cols])
            upper = ((trow >> m_log) & 1) == 1
            kt = jnp.where(upper, 0.0, k_p * e).astype(BF16)
            qt = jnp.where(upper, q_p * e, 0.0)
            same = (ti >> (m_log + 1)) == (si >> (m_log + 1))
            for hh in range(2):
                al = _dot_nt(jnp.where(head_lanes[hh], qt, 0.0).astype(BF16), kt)
                a[hh] = a[hh] + jnp.where(same, al, 0.0)

        u = []
        for hh in range(2):
            h = 2 * p + hh
            v_h = v_ref[:, h * GLA_DV:(h + 1) * GLA_DV].astype(BF16)
            o_h = _dot(a[hh].astype(BF16), v_h) + _dot(jnp.where(head_lanes[hh], qe_p, 0.0).astype(BF16), s_pair_b)
            o_ref[:, h * GLA_DV:(h + 1) * GLA_DV] = o_h
            u.append(_dot_tn(kd[:, cols], v_h))
        bl_rows = _dot_tn(la_hi[:, cols], ones_cw) + _dot_tn(la_lo[:, cols], ones_cw)
        s_ref[cols, :] = jnp.exp(bl_rows[:, :GLA_DV]) * s_pair + jnp.where(srow < GLA_DK, u[0], u[1])

    @pl.when(ci == pl.num_programs(1) - 1)
    def _():
        sout_ref[...] = s_ref[...]


def _gla(q, k, v, la, s0, c):
    bsz, l, _ = q.shape
    assert l % c == 0 and (c & (c - 1)) == 0 and GLA_DV == 2 * GLA_DK
    lvl = _gla_level_matrix(c)
    tok = lambda b, i: (b, i, 0)
    per_b = lambda b, i: (b, 0, 0)
    return pl.pallas_call(
        functools.partial(_gla_kernel, c=c),
        grid=(bsz, l // c),
        in_specs=[
            pl.BlockSpec((None, c, GLA_QK_W), tok),
            pl.BlockSpec((None, c, GLA_QK_W), tok),
            pl.BlockSpec((None, c, GLA_V_W), tok),
            pl.BlockSpec((None, c, GLA_QK_W), tok),
            pl.BlockSpec((None, GLA_QK_W, GLA_DV), per_b),
            pl.BlockSpec(lvl.shape, lambda b, i: (0, 0)),
        ],
        out_specs=[pl.BlockSpec((None, c, GLA_V_W), tok), pl.BlockSpec((None, GLA_QK_W, GLA_DV), per_b)],
        out_shape=[jax.ShapeDtypeStruct((bsz, l, GLA_V_W), F32), jax.ShapeDtypeStruct((bsz, GLA_QK_W, GLA_DV), F32)],
        scratch_shapes=[pltpu.VMEM((GLA_QK_W, GLA_DV), F32)],
        compiler_params=_params(2),
        name="gla",
    )(q, k, v, la, s0, lvl)


def _select_topk_blocks(gate, n_valid):
    row = lax.broadcasted_iota(jnp.int32, gate.shape, 0)
    g = jnp.where(row < n_valid, gate, NEG)
    sel = jnp.zeros(gate.shape, F32)
    for r in range(MOBA_TOPK):
        mx = jnp.max(g, axis=0, keepdims=True)
        idx = jnp.min(jnp.where(g == mx, row, gate.shape[0]), axis=0, keepdims=True)
        pick = row == jnp.where(r < n_valid, idx, -1)
        sel = jnp.where(pick, 1.0, sel)
        g = jnp.where(pick, NEG, g)
    return sel


def _moba_prompt_kernel(q_ref, k_ref, v_ref, o_ref, kmean_ref, sel_ref, sa_ref, sb_ref, *, nb):
    qi = pl.program_id(2)
    blk = MOBA_BLOCK
    hd = MOBA_HD
    heads = range(2)

    def blk_ds(n):
        return pl.ds(pl.multiple_of(n * blk, blk), blk)

    @pl.when(qi == 0)
    def _():
        kmean_ref[...] = jnp.zeros(kmean_ref.shape, F32)

        def body(n, carry):
            kb = k_ref[blk_ds(n), :].astype(F32)
            kmean_ref[pl.ds(n, 1), :] = jnp.sum(kb, axis=0, keepdims=True) * (1.0 / blk)
            return carry

        lax.fori_loop(0, nb, body, 0)

    row = lax.broadcasted_iota(jnp.int32, (2 * hd, blk), 0)
    q_both = q_ref[...]
    q_pair = [jnp.where((row // hd) == hh, q_both, jnp.zeros((), BF16)) for hh in heads]
    km_hi, km_lo = _split2(kmean_ref[...])
    for hh in heads:
        gate = _dot(km_hi, q_pair[hh]) + _dot(km_lo, q_pair[hh])
        sel_ref[hh] = _select_topk_blocks(gate, qi)

    ones_rows = jnp.where(lax.broadcasted_iota(jnp.int32, (16, blk), 0) == 0, 1.0, 0.0).astype(BF16)

    def v_aug(hh, n):
        return jnp.concatenate([v_ref[hh * hd:(hh + 1) * hd, blk_ds(n)].astype(BF16), ones_rows], axis=0)

    def update(carry, scores, v_tiles, picked):
        m, acc = carry
        m_new = m
        for s, pk in zip(scores, picked):
            bm = jnp.max(s, axis=0, keepdims=True)
            m_new = jnp.maximum(m_new, bm if pk is None else jnp.where(pk, bm, NEG))
        acc = acc * jnp.exp2(m - m_new)
        for s, v_t, pk in zip(scores, v_tiles, picked):
            c = _dot(v_t, jnp.exp2(s - m_new).astype(BF16))
            acc = acc + (c if pk is None else jnp.where(pk, c, 0.0))
        return m_new, acc

    half = MOBA_GROUP // 2

    def clamp(n):
        return jnp.minimum(n, nb - 1)

    def stage_scores(dst_ref, first):
        for hh in heads:
            for g in range(half):
                dst_ref[hh * half + g] = _dot(k_ref[blk_ds(clamp(first + g)), :], q_pair[hh])

    def stage_update(src_ref, first, carries):
        out = []
        for hh in heads:
            blocks = [clamp(first + g) for g in range(half)]
            scores = [src_ref[hh * half + g] for g in range(half)]
            picked = [sel_ref[hh, pl.ds(n, 1), :] > 0.5 for n in blocks]
            out.append(update(carries[hh], scores, [v_aug(hh, n) for n in blocks], picked))
        return tuple(out)

    def past_group(gi, carries):
        base = gi * MOBA_GROUP
        stage_scores(sb_ref, base + half)
        carries = stage_update(sa_ref, base, carries)
        stage_scores(sa_ref, base + MOBA_GROUP)
        return stage_update(sb_ref, base + half, carries)

    causal = lax.broadcasted_iota(jnp.int32, (blk, blk), 0) <= lax.broadcasted_iota(jnp.int32, (blk, blk), 1)
    k_own = k_ref[blk_ds(qi), :]
    init = (jnp.full((1, blk), NEG, F32), jnp.zeros((hd + 16, blk), F32))
    carries = tuple(update(init, [jnp.where(causal, _dot(k_own, q_pair[hh]), NEG)], [v_aug(hh, qi)], [None])
                    for hh in heads)
    stage_scores(sa_ref, 0)
    carries = lax.fori_loop(0, (qi + MOBA_GROUP - 1) // MOBA_GROUP, past_group, carries)
    o_ref[...] = jnp.concatenate([jnp.transpose(acc[:hd] / acc[hd:hd + 1]) for _, acc in carries], axis=1)


def _moba_prompt(qt, k_nat, vt):
    bsz, _, l = qt.shape
    blk = MOBA_BLOCK
    pair_w = 2 * MOBA_HD
    assert l % blk == 0 and pair_w == LANES
    nb = l // blk
    nb_pad = -(-nb // SUBLANES) * SUBLANES
    return pl.pallas_call(
        functools.partial(_moba_prompt_kernel, nb=nb),
        grid=(bsz, MOBA_HEADS // 2, nb),
        in_specs=[
            pl.BlockSpec((None, pair_w, blk), lambda b, p, i: (b, p, i)),
            pl.BlockSpec((None, l, pair_w), lambda b, p, i: (b, 0, p)),
            pl.BlockSpec((None, pair_w, l), lambda b, p, i: (b, p, 0)),
        ],
        out_specs=pl.BlockSpec((None, blk, pair_w), lambda b, p, i: (b, i, p)),
        out_shape=jax.ShapeDtypeStruct((bsz, l, MOBA_W), F32),
        scratch_shapes=[pltpu.VMEM((nb_pad, pair_w), F32), pltpu.VMEM((2, nb_pad, blk), F32),
                        pltpu.VMEM((MOBA_GROUP, blk, blk), F32), pltpu.VMEM((MOBA_GROUP, blk, blk), F32)],
        compiler_params=_params(3),
        name="moba_prompt",
    )(qt, k_nat, vt)


def _select_topk_lanes(gate, n_valid):
    lane = lax.broadcasted_iota(jnp.int32, gate.shape, 1)
    g = jnp.where(lane < n_valid, gate, NEG)
    sel = jnp.zeros(gate.shape, F32)
    for r in range(min(MOBA_TOPK, n_valid)):
        mx = jnp.max(g, axis=1, keepdims=True)
        idx = jnp.min(jnp.where(g == mx, lane, gate.shape[1]), axis=1, keepdims=True)
        pick = lane == idx
        sel = jnp.where(pick, 1.0, sel)
        g = jnp.where(pick, NEG, g)
    return sel


def _moba_sample_kernel(pt_ref, qb_ref, kn_ref, vn_ref, *refs, npp, n_steps, n_pages, page, t_new):
    k_refs = refs[:npp]
    v_refs = refs[npp:2 * npp]
    o_ref = refs[2 * npp]
    s_ref, p_ref, acc_ref = refs[2 * npp + 1:]
    j = pl.program_id(1)
    blk = MOBA_BLOCK
    n_blk = (n_pages * page) // blk
    past = n_pages * page
    n_c = qb_ref.shape[0]
    t_pad = n_c // MOBA_HEADS

    def flat(ref):
        return ref[...].reshape(MOBA_W, page)

    @pl.when(j < n_steps)
    def _():
        qb = qb_ref[...]
        for i in range(0, npp, 2):
            w = jnp.concatenate([flat(k_refs[i]), flat(k_refs[i + 1])], axis=1).astype(BF16)
            cols = pl.ds(pl.multiple_of((j * npp + i) * page, 2 * page), 2 * page)
            s_ref[:, cols] = _dot(qb, w)

    @pl.when(j == n_steps - 1)
    def _():
        qb = qb_ref[...]
        lane = lax.broadcasted_iota(jnp.int32, (n_c, LANES), 1)

        def blk_cols(n):
            return pl.ds(pl.multiple_of(n * blk, blk), blk)

        def gate_body(n, gate):
            gsum = jnp.sum(s_ref[:, blk_cols(n)], axis=1, keepdims=True) * (1.0 / blk)
            return jnp.where(lane == n, gsum, gate)

        unroll = math.gcd(n_blk, 8)
        gate = lax.fori_loop(0, n_blk, gate_body, jnp.zeros((n_c, LANES), F32), unroll=unroll)
        sel = _select_topk_lanes(gate, n_blk)

        s_own = _dot(qb, kn_ref[...].astype(BF16))
        key_t = lax.broadcasted_iota(jnp.int32, (n_c, page), 1)
        row_t = lax.broadcasted_iota(jnp.int32, (n_c, page), 0) % t_pad
        s_own = jnp.where((key_t < t_new) & (key_t <= row_t), s_own, NEG)

        def masked(n):
            picked = jnp.sum(jnp.where(lane == n, sel, 0.0), axis=1, keepdims=True) > 0.5
            return jnp.where(picked, s_ref[:, blk_cols(n)], NEG)

        def max_body(n, m):
            return jnp.maximum(m, masked(n))

        m_run = lax.fori_loop(0, n_blk, max_body, jnp.full((n_c, blk), NEG, F32), unroll=unroll)
        m = jnp.maximum(jnp.max(m_run, axis=1, keepdims=True), jnp.max(s_own, axis=1, keepdims=True))

        def exp_body(n, l_run):
            e = jnp.exp(masked(n) - m)
            s_ref[:, blk_cols(n)] = e
            return l_run + e

        l_run = lax.fori_loop(0, n_blk, exp_body, jnp.zeros((n_c, blk), F32), unroll=unroll)
        e_own = jnp.exp(s_own - m)
        inv = 1.0 / (jnp.sum(l_run, axis=1, keepdims=True) + jnp.sum(e_own, axis=1, keepdims=True))

        def norm_body(n, carry):
            p_ref[:, blk_cols(n)] = (s_ref[:, blk_cols(n)] * inv).astype(BF16)
            return carry

        lax.fori_loop(0, n_blk, norm_body, 0, unroll=unroll)
        p_ref[:, past:past + page] = (e_own * inv).astype(BF16)
        acc_ref[...] = jnp.zeros(acc_ref.shape, F32)

    @pl.when(j >= n_steps)
    def _():
        v_cat = jnp.concatenate([flat(v_refs[i]) for i in range(npp)], axis=1).astype(BF16)
        cols = pl.ds(pl.multiple_of((j - n_steps) * (npp * page), npp * page), npp * page)
        acc_ref[...] += _dot_nt(v_cat, p_ref[:, cols])

    @pl.when(j == 2 * n_steps - 1)
    def _():
        o_ref[...] = acc_ref[...] + _dot_nt(vn_ref[...].astype(BF16), p_ref[:, past:past + page])


def _moba_sample(mq, mk, mv, cache_kt, cache_vt, layer, page_table):
    dbs, t_new, _ = mq.shape
    page = cache_kt.shape[-1]
    n_pages = page_table.shape[1]
    npp = SAMPLE_PAGES_PER_STEP
    t_pad = t_new + t_new % 2
    n_c = MOBA_HEADS * t_pad
    assert n_pages % npp == 0 and npp % 2 == 0 and page == LANES
    assert (n_pages * page) % MOBA_BLOCK == 0 and (n_pages * page) // MOBA_BLOCK <= LANES and t_new <= page
    n_steps = n_pages // npp

    q4 = (mq * (MOBA_HD ** -0.5)).reshape(dbs, t_new, MOBA_HEADS, MOBA_HD)
    q4 = jnp.pad(q4, ((0, 0), (0, t_pad - t_new), (0, 0), (0, 0)))
    eye = jnp.eye(MOBA_HEADS, dtype=F32)
    qb = (q4[:, :, :, None, :] * eye[None, None, :, :, None]).transpose(0, 2, 1, 3, 4)
    qb = qb.reshape(dbs, n_c, MOBA_W).astype(BF16)
    kn = jnp.pad(jnp.transpose(mk, (0, 2, 1)), ((0, 0), (0, 0), (0, page - t_new)))
    vn = jnp.pad(jnp.transpose(mv, (0, 2, 1)), ((0, 0), (0, 0), (0, page - t_new)))
    pt_flat = page_table.reshape(-1).astype(jnp.int32)

    def k_map(i):
        return lambda b, j, pt: (layer, pt[b * n_pages + jnp.minimum(j, n_steps - 1) * npp + i], 0, 0, 0)

    def v_map(i):
        return lambda b, j, pt: (layer, pt[b * n_pages + jnp.maximum(j - n_steps, 0) * npp + i], 0, 0, 0)

    per_b = lambda b, j, pt: (b, 0, 0)
    page_block = (None, None, MOBA_HEADS, MOBA_HD, page)
    grid_spec = pltpu.PrefetchScalarGridSpec(
        num_scalar_prefetch=1,
        grid=(dbs, 2 * n_steps),
        in_specs=[pl.BlockSpec((None, n_c, MOBA_W), per_b),
                  pl.BlockSpec((None, MOBA_W, page), per_b),
                  pl.BlockSpec((None, MOBA_W, page), per_b)]
                 + [pl.BlockSpec(page_block, k_map(i)) for i in range(npp)]
                 + [pl.BlockSpec(page_block, v_map(i)) for i in range(npp)],
        out_specs=pl.BlockSpec((None, MOBA_W, n_c), per_b),
        scratch_shapes=[
            pltpu.VMEM((n_c, n_pages * page), F32),
            pltpu.VMEM((n_c, (n_pages + 1) * page), BF16),
            pltpu.VMEM((MOBA_W, n_c), F32),
        ],
    )
    out_t = pl.pallas_call(
        functools.partial(_moba_sample_kernel, npp=npp, n_steps=n_steps, n_pages=n_pages, page=page, t_new=t_new),
        grid_spec=grid_spec,
        out_shape=jax.ShapeDtypeStruct((dbs, MOBA_W, n_c), F32),
        compiler_params=_params(2),
        name="moba_sample",
    )(pt_flat, qb, kn, vn, *([cache_kt] * npp), *([cache_vt] * npp))
    o5 = out_t.reshape(dbs, MOBA_HEADS, MOBA_HD, MOBA_HEADS, t_pad)
    o = jnp.sum(o5 * eye[None, :, None, :, None], axis=3)
    return o.transpose(0, 3, 1, 2)[:, :t_new].reshape(dbs, t_new, MOBA_W)


def _outproj_kernel(x_ref, go_ref, gr_ref, mo_ref, ggo_ref, wo1_ref, wo2_ref, gffn_ref, wr_ref, br_ref,
                    h_ref, xn_ref, eid_ref, wt_ref, rank_ref, cnt_ref, carry_ref):
    i = pl.program_id(0)
    t = x_ref.shape[0]

    @pl.when(i == 0)
    def _():
        carry_ref[...] = jnp.zeros(carry_ref.shape, F32)

    gr = gr_ref[...]
    gate = gr / (1.0 + jnp.exp(-gr))
    parts = []
    for hd in range(GLA_HEADS):
        cols = slice(hd * GLA_DV, (hd + 1) * GLA_DV)
        o = go_ref[:, cols]
        ms = jnp.mean(o * o, axis=-1, keepdims=True)
        parts.append(o * lax.rsqrt(ms + EPS) * ggo_ref[...] * gate[:, cols])
    a1 = jnp.concatenate(parts, axis=1).astype(BF16)
    h = x_ref[...] + _dot(a1, wo1_ref[...]) + _dot(mo_ref[...].astype(BF16), wo2_ref[...])
    h_ref[...] = h

    ms = jnp.mean(h * h, axis=-1, keepdims=True)
    xn = h * lax.rsqrt(ms + EPS) * gffn_ref[...]
    xn_ref[...] = xn
    logits = _dot_nt(wr_ref[...], xn.astype(BF16)) + br_ref[...]

    gl = logits[0:N_GROUPS]
    grow = lax.broadcasted_iota(jnp.int32, gl.shape, 0)
    gmax = jnp.max(gl, axis=0, keepdims=True)
    g_sel = jnp.min(jnp.where(gl == gmax, grow, N_GROUPS), axis=0, keepdims=True)
    gw = 1.0 / jnp.sum(jnp.exp(gl - gmax), axis=0, keepdims=True)

    el = logits[SUBLANES:SUBLANES + N_EXPERTS]
    erow = lax.broadcasted_iota(jnp.int32, el.shape, 0)
    in_grp = (erow // EXPERTS_PER_GROUP) == g_sel
    emax = jnp.max(jnp.where(in_grp, el, NEG), axis=0, keepdims=True)
    ee = jnp.where(in_grp, jnp.exp(el - emax), 0.0)
    ep = ee / jnp.sum(ee, axis=0, keepdims=True)
    cand = jnp.where(in_grp, ep, -1.0)
    p1 = jnp.max(cand, axis=0, keepdims=True)
    i1 = jnp.min(jnp.where(cand == p1, erow, N_EXPERTS), axis=0, keepdims=True)
    cand = jnp.where(erow == i1, -1.0, cand)
    p2 = jnp.max(cand, axis=0, keepdims=True)
    i2 = jnp.min(jnp.where(cand == p2, erow, N_EXPERTS), axis=0, keepdims=True)
    denom = p1 + p2
    zero_i = jnp.zeros((SUBLANES - EXPERT_TOPK, t), jnp.int32)
    zero_f = jnp.zeros((SUBLANES - EXPERT_TOPK, t), F32)
    eid_ref[...] = jnp.concatenate([i1, i2, zero_i], axis=0)
    wt_ref[...] = jnp.concatenate([p1 / denom * gw, p2 / denom * gw, zero_f], axis=0)

    oh1 = erow == i1
    oh2 = erow == i2
    hits = jnp.where(oh1 | oh2, 1.0, 0.0)
    before = (lax.broadcasted_iota(jnp.int32, (t, t), 0) < lax.broadcasted_iota(jnp.int32, (t, t), 1))
    prefix = _dot(hits.astype(BF16), jnp.where(before, 1.0, 0.0).astype(BF16)) + carry_ref[...]
    r1 = jnp.sum(jnp.where(oh1, prefix, 0.0), axis=0, keepdims=True)
    r2 = jnp.sum(jnp.where(oh2, prefix, 0.0), axis=0, keepdims=True)
    rank_ref[...] = jnp.concatenate([r1.astype(jnp.int32), r2.astype(jnp.int32), zero_i], axis=0)
    carry_ref[...] += jnp.sum(hits, axis=1, keepdims=True)
    cnt_ref[...] = jnp.broadcast_to(carry_ref[...], cnt_ref.shape)


def _outproj(x2d, go, gr, mo, w):
    n, d = x2d.shape
    t = ROW_TILE
    assert n % t == 0
    row = lambda i: (i, 0)
    col = lambda i: (0, i)
    fixed = lambda i: (0, 0)
    return pl.pallas_call(
        _outproj_kernel,
        grid=(n // t,),
        in_specs=[
            pl.BlockSpec((t, d), row),
            pl.BlockSpec((t, GLA_V_W), row),
            pl.BlockSpec((t, GLA_V_W), row),
            pl.BlockSpec((t, MOBA_W), row),
            pl.BlockSpec((1, GLA_DV), fixed),
            pl.BlockSpec(w["wo1"].shape, fixed),
            pl.BlockSpec(w["wo2"].shape, fixed),
            pl.BlockSpec((1, d), fixed),
            pl.BlockSpec(w["wr"].shape, fixed),
            pl.BlockSpec(w["br"].shape, fixed),
        ],
        out_specs=[pl.BlockSpec((t, d), row), pl.BlockSpec((t, d), row),
                   pl.BlockSpec((SUBLANES, t), col), pl.BlockSpec((SUBLANES, t), col),
                   pl.BlockSpec((SUBLANES, t), col), pl.BlockSpec((N_EXPERTS, LANES), fixed)],
        out_shape=[jax.ShapeDtypeStruct((n, d), F32), jax.ShapeDtypeStruct((n, d), F32),
                   jax.ShapeDtypeStruct((SUBLANES, n), jnp.int32), jax.ShapeDtypeStruct((SUBLANES, n), F32),
                   jax.ShapeDtypeStruct((SUBLANES, n), jnp.int32), jax.ShapeDtypeStruct((N_EXPERTS, LANES), F32)],
        scratch_shapes=[pltpu.VMEM((N_EXPERTS, 1), F32)],
        compiler_params=_params(1),
        name="outproj",
    )(x2d, go, gr, mo, w["g_gla_out"], w["wo1"], w["wo2"], w["g_ffn"], w["wr"], w["br"])


def _row_copy(src_ref, src_row, dst_ref, dst_row, sem):
    return pltpu.make_async_copy(src_ref.at[pl.ds(src_row, 1)], dst_ref.at[pl.ds(dst_row, 1)], sem)


def _moe_scatter_kernel(dest_ref, x_ref, xs_in_ref, xs_ref, sem):
    del xs_in_ref
    t = x_ref.shape[0]

    def start(r, carry):
        for kk in range(EXPERT_TOPK):
            _row_copy(x_ref, r, xs_ref, dest_ref[kk, r], sem).start(priority=kk % 2)
        return carry

    def wait(r, carry):
        for kk in range(EXPERT_TOPK):
            _row_copy(x_ref, 0, xs_ref, 0, sem).wait()
        return carry

    lax.fori_loop(0, t, start, 0, unroll=ROW_DMA_UNROLL)
    lax.fori_loop(0, t, wait, 0, unroll=ROW_DMA_UNROLL)


def _moe_scatter(xn, dest, n_rows):
    n, d = xn.shape
    t = ROW_TILE
    xs0 = jnp.zeros((n_rows, d), F32)
    return pl.pallas_call(
        _moe_scatter_kernel,
        grid=(n // t,),
        in_specs=[
            pl.BlockSpec((EXPERT_TOPK, t), lambda i: (0, i), memory_space=pltpu.SMEM),
            pl.BlockSpec((t, d), lambda i: (i, 0)),
            pl.BlockSpec(memory_space=pl.ANY),
        ],
        out_specs=pl.BlockSpec(memory_space=pl.ANY),
        out_shape=jax.ShapeDtypeStruct((n_rows, d), F32),
        scratch_shapes=[pltpu.SemaphoreType.DMA(())],
        input_output_aliases={2: 0},
        compiler_params=_params(1),
        name="moe_scatter",
    )(dest, xn, xs0)


def _moe_ffn_kernel(be_ref, nu_ref, xs_ref, wg_ref, wu_ref, wd_ref, ys_ref):
    i = pl.program_id(0)

    @pl.when(i < nu_ref[0])
    def _():
        x = xs_ref[...].astype(BF16)
        g = _dot(x, wg_ref[...])
        u = _dot(x, wu_ref[...])
        hmid = (g / (1.0 + jnp.exp(-g)) * u).astype(BF16)
        ys_ref[...] = _dot(hmid, wd_ref[...])

    @pl.when(i >= nu_ref[0])
    def _():
        ys_ref[...] = jnp.zeros(ys_ref.shape, F32)


def _moe_ffn(xs, blk_expert, n_used, wg, wu, wd):
    r, d = xs.shape
    de = wg.shape[2]
    n_blk = r // MOE_BLOCK
    rows = lambda i, be, nu: (jnp.minimum(i, nu[0] - 1), 0)
    wsel = lambda i, be, nu: (be[jnp.minimum(i, nu[0] - 1)], 0, 0)
    grid_spec = pltpu.PrefetchScalarGridSpec(
        num_scalar_prefetch=2,
        grid=(n_blk,),
        in_specs=[
            pl.BlockSpec((MOE_BLOCK, d), rows),
            pl.BlockSpec((None, d, de), wsel),
            pl.BlockSpec((None, d, de), wsel),
            pl.BlockSpec((None, de, d), wsel),
        ],
        out_specs=pl.BlockSpec((MOE_BLOCK, d), lambda i, be, nu: (i, 0)),
    )
    return pl.pallas_call(
        _moe_ffn_kernel,
        grid_spec=grid_spec,
        out_shape=jax.ShapeDtypeStruct((r, d), F32),
        compiler_params=_params(1),
        name="moe_ffn",
    )(blk_expert, n_used, xs, wg, wu, wd)


def _moe_combine_kernel(dest_ref, h_ref, wt_ref, ys_ref, y_ref, buf_ref, sem):
    t = h_ref.shape[0]

    def start(r, carry):
        for kk in range(EXPERT_TOPK):
            _row_copy(ys_ref, dest_ref[kk, r], buf_ref.at[kk], r, sem).start(priority=kk % 2)
        return carry

    def wait(r, carry):
        for kk in range(EXPERT_TOPK):
            _row_copy(ys_ref, 0, buf_ref.at[kk], 0, sem).wait()
        return carry

    lax.fori_loop(0, t, start, 0, unroll=ROW_DMA_UNROLL)
    lax.fori_loop(0, t, wait, 0, unroll=ROW_DMA_UNROLL)
    y = h_ref[...]
    for kk in range(EXPERT_TOPK):
        y = y + buf_ref[kk] * wt_ref[:, kk:kk + 1]
    y_ref[...] = y


def _moe_combine(h, wt_rows, dest, ys):
    n, d = h.shape
    t = ROW_TILE
    return pl.pallas_call(
        _moe_combine_kernel,
        grid=(n // t,),
        in_specs=[
            pl.BlockSpec((EXPERT_TOPK, t), lambda i: (0, i), memory_space=pltpu.SMEM),
            pl.BlockSpec((t, d), lambda i: (i, 0)),
            pl.BlockSpec((t, SUBLANES), lambda i: (i, 0)),
            pl.BlockSpec(memory_space=pl.ANY),
        ],
        out_specs=pl.BlockSpec((t, d), lambda i: (i, 0)),
        out_shape=jax.ShapeDtypeStruct((n, d), F32),
        scratch_shapes=[pltpu.VMEM((EXPERT_TOPK, t, d), F32), pltpu.SemaphoreType.DMA(())],
        compiler_params=_params(1),
        name="moe_combine",
    )(dest, h, wt_rows, ys)


def _moe(h, xn, eid, wt, rank, counts, w):
    n, d = h.shape
    blk = MOE_BLOCK
    n_blk = -(-(n * EXPERT_TOPK) // blk) + N_EXPERTS
    cnt = counts[:, 0].astype(jnp.int32)
    padded = (cnt + blk - 1) // blk * blk
    pend = jnp.cumsum(padded)
    pstart = pend - padded
    dest = rank[:EXPERT_TOPK]
    for e in range(N_EXPERTS):
        dest = dest + jnp.where(eid[:EXPERT_TOPK] == e, pstart[e], 0)
    blk_expert = jnp.minimum(jnp.searchsorted(pend, jnp.arange(n_blk) * blk, side="right"),
                             N_EXPERTS - 1).astype(jnp.int32)
    n_used = (pend[-1:] // blk).astype(jnp.int32)

    xs = _moe_scatter(xn, dest, n_blk * blk)
    ys = _moe_ffn(xs, blk_expert, n_used, w["w_gate"], w["w_up"], w["w_down"])
    return _moe_combine(h, jnp.transpose(wt), dest, ys)


def _prep_weights(g_attn_norm, w_in, w_gla_gate_up, b_gla_gate, g_gla_out, g_q, g_k, w_out, g_ffn_norm,
                  w_group_router, b_group_router, w_expert_router, b_expert_router, w_gate, w_up, w_down):
    d = w_in.shape[0]
    o = np.cumsum((GLA_QK_W, GLA_QK_W, GLA_V_W, GLA_V_W, GLA_GATE_RANK, MOBA_W, MOBA_W, MOBA_W))
    wlr = jnp.pad(w_in[:, o[3]:o[4]], ((0, 0), (0, LANES - GLA_GATE_RANK)))
    wup = jnp.pad(w_gla_gate_up, ((0, LANES - GLA_GATE_RANK), (0, 0)))
    wup_hi = wup.astype(BF16)
    wup_lo = (wup - wup_hi.astype(F32)).astype(BF16)
    wr = jnp.zeros((4 * SUBLANES, d), F32)
    wr = wr.at[0:N_GROUPS].set(w_group_router.T).at[SUBLANES:SUBLANES + N_EXPERTS].set(w_expert_router.T)
    br = jnp.zeros((4 * SUBLANES, 1), F32)
    br = br.at[0:N_GROUPS, 0].set(b_group_router).at[SUBLANES:SUBLANES + N_EXPERTS, 0].set(b_expert_router)
    return {
        "g_attn": g_attn_norm[None, :],
        "wg": w_in[:, :o[3]].astype(BF16),
        "wlr": wlr.astype(BF16),
        "wup": jnp.stack([wup_hi, wup_lo]),
        "b_gate": b_gla_gate[None, :],
        "wm": w_in[:, o[4]:].astype(BF16),
        "g_q": jnp.tile(g_q, MOBA_HEADS)[None, :],
        "g_k": jnp.tile(g_k, MOBA_HEADS)[None, :],
        "g_gla_out": g_gla_out[None, :],
        "wo1": w_out[:GLA_V_W].astype(BF16),
        "wo2": w_out[GLA_V_W:].astype(BF16),
        "g_ffn": g_ffn_norm[None, :],
        "wr": wr.astype(BF16),
        "br": br,
        "w_gate": w_gate.astype(BF16),
        "w_up": w_up.astype(BF16),
        "w_down": w_down.astype(BF16),
    }


def _layer(x, positions, s0, gla_chunk, sample_moba, w):
    bsz, l, d = x.shape
    n = bsz * l
    x2d = x.reshape(n, d)
    cos_tab, sin_tab = _rope_tables(positions)
    if sample_moba is None:
        gq, gk, gv, gr, la, qt, kt, vt, k_nat = _inproj(x2d, w, cos_tab, sin_tab, seq_len=l)
        mo = _moba_prompt(qt, k_nat, vt).reshape(n, MOBA_W)
        k_new = kt.reshape(bsz, MOBA_HEADS, MOBA_HD, l).transpose(0, 3, 1, 2)
        v_new = vt.reshape(bsz, MOBA_HEADS, MOBA_HD, l).transpose(0, 3, 1, 2)
    else:
        gq, gk, gv, gr, la, mq, mk, mv = _inproj(x2d, w, cos_tab, sin_tab)
        mk3 = mk.reshape(bsz, l, MOBA_W)
        mv3 = mv.reshape(bsz, l, MOBA_W)
        mo = sample_moba(mq.reshape(bsz, l, MOBA_W), mk3, mv3).reshape(n, MOBA_W)
        k_new = mk3.reshape(bsz, l, MOBA_HEADS, MOBA_HD)
        v_new = mv3.reshape(bsz, l, MOBA_HEADS, MOBA_HD)

    lp = -(-l // gla_chunk) * gla_chunk
    seq = lambda a: jnp.pad(a.reshape(bsz, l, -1), ((0, 0), (0, lp - l), (0, 0)))
    go, s_new = _gla(seq(gq), seq(gk), seq(gv), seq(la), s0.reshape(bsz, GLA_QK_W, GLA_DV), gla_chunk)
    go = go[:, :l].reshape(n, GLA_V_W)

    h, xn, eid, wt, rank, counts = _outproj(x2d, go, gr, mo, w)
    y = _moe(h, xn, eid, wt, rank, counts, w)
    return y.reshape(bsz, l, d), k_new, v_new, s_new.reshape(bsz, GLA_HEADS, GLA_DK, GLA_DV)


def kernel(x_prompt, x_sample, cache_k, cache_v, state_gla, page_table, g_attn_norm, w_in, w_gla_gate_up, b_gla_gate,
           g_gla_out, g_q, g_k, w_out, g_ffn_norm, w_group_router, b_group_router, w_expert_router, b_expert_router,
           w_gate, w_up, w_down):
    depth = w_in.shape[0]
    bsz, l, _ = x_prompt.shape
    dbs, t_new, _ = x_sample.shape
    n_phys, page = cache_k.shape[1], cache_k.shape[2]
    past = page_table.shape[1] * page
    assert l % ROW_TILE == 0 and (dbs * t_new) % ROW_TILE == 0 and ROW_TILE % t_new == 0

    pos_p = np.arange(l)
    pos_s = past + (np.arange(ROW_TILE) % t_new)
    chunk_s = max(SUBLANES, 1 << (t_new - 1).bit_length())
    cache_kt = jnp.transpose(cache_k, (0, 1, 3, 4, 2))
    cache_vt = jnp.transpose(cache_v, (0, 1, 3, 4, 2))

    hp, hs = x_prompt, x_sample
    outs = [[] for _ in range(6)]
    for li in range(depth):
        w = _prep_weights(g_attn_norm[li], w_in[li], w_gla_gate_up[li], b_gla_gate[li], g_gla_out[li], g_q[li],
                          g_k[li], w_out[li], g_ffn_norm[li], w_group_router[li], b_group_router[li],
                          w_expert_router[li], b_expert_router[li], w_gate[li], w_up[li], w_down[li])
        s0 = jnp.zeros((bsz, GLA_HEADS, GLA_DK, GLA_DV), F32)
        hp, kp, vp, sp = _layer(hp, pos_p, s0, min(GLA_CHUNK, l), None, w)

        moba_s = lambda q, k, v, li=li: _moba_sample(q, k, v, cache_kt, cache_vt, li, page_table)
        hs, ks, vs, ss = _layer(hs, pos_s, state_gla[li].astype(F32), chunk_s, moba_s, w)
        for lst, val in zip(outs, (kp, vp, sp.astype(state_gla.dtype), ks, vs, ss.astype(state_gla.dtype))):
            lst.append(val)

    return (hp, hs) + tuple(jnp.stack(o) for o in outs)
```

```python
import functools
import math

import jax
import jax.numpy as jnp
import numpy as np
from jax import lax
from jax.experimental import pallas as pl
from jax.experimental.pallas import tpu as pltpu

GLA_HEADS = 4
GLA_DK = 64
GLA_DV = 128
GLA_GATE_RANK = 16
GLA_TAU = 16.0
MOBA_HEADS = 8
MOBA_HD = 64
MOBA_BLOCK = 256
MOBA_TOPK = 3
ROPE_THETA = 10000.0
N_GROUPS = 4
EXPERTS_PER_GROUP = 4
N_EXPERTS = N_GROUPS * EXPERTS_PER_GROUP
EXPERT_TOPK = 2
EPS = 1e-6

GLA_QK_W = GLA_HEADS * GLA_DK
GLA_V_W = GLA_HEADS * GLA_DV
MOBA_W = MOBA_HEADS * MOBA_HD

LANES = 128
SUBLANES = 8
VMEM_LIMIT_BYTES = 56 * 1024 * 1024

ROW_TILE = 256
GLA_CHUNK = 256
MOE_BLOCK = 256
SAMPLE_PAGES_PER_STEP = 16
ROW_DMA_UNROLL = 8
MOBA_GROUP = 4
F32 = jnp.float32
BF16 = jnp.bfloat16
NEG = -1e30
LOG2E = math.log2(math.e)


def _dot(a, b):
    return jnp.dot(a, b, preferred_element_type=F32)


def _dot_nt(a, b):
    return lax.dot_general(a, b, (((1,), (1,)), ((), ())), preferred_element_type=F32)


def _dot_tn(a, b):
    return lax.dot_general(a, b, (((0,), (0,)), ((), ())), preferred_element_type=F32)


def _split2(x):
    hi = x.astype(BF16)
    lo = (x - hi.astype(F32)).astype(BF16)
    return hi, lo


def _params(n_axes):
    return pltpu.CompilerParams(dimension_semantics=("arbitrary",) * n_axes,
                                vmem_limit_bytes=VMEM_LIMIT_BYTES)


def _head_norm(y, g):
    lo = lax.broadcasted_iota(jnp.int32, (1, LANES), 1) < MOBA_HD
    outs = []
    for p in range(MOBA_W // LANES):
        yp = y[:, p * LANES:(p + 1) * LANES]
        sq = yp * yp
        s_lo = jnp.sum(jnp.where(lo, sq, 0.0), axis=-1, keepdims=True)
        s_hi = jnp.sum(jnp.where(lo, 0.0, sq), axis=-1, keepdims=True)
        ms = jnp.where(lo, s_lo, s_hi) * (1.0 / MOBA_HD)
        outs.append(yp * lax.rsqrt(ms + EPS))
    return jnp.concatenate(outs, axis=1) * g


def _rope(y, cos, sin_signed):
    n = y.shape[1]
    first = (lax.broadcasted_iota(jnp.int32, (1, n), 1) % MOBA_HD) < (MOBA_HD // 2)
    rot = jnp.where(first, pltpu.roll(y, n - MOBA_HD // 2, 1), pltpu.roll(y, MOBA_HD // 2, 1))
    return y * cos + rot * sin_signed


def _inproj_kernel(x_ref, gin_ref, wg_ref, wlr_ref, wup_ref, bgate_ref, wm_ref, gq_ref, gk_ref, cos_ref, sin_ref,
                   oq_ref, ok_ref, ov_ref, or_ref, ola_ref, *moba_refs, transposed):
    x = x_ref[...]
    ms = jnp.mean(x * x, axis=-1, keepdims=True)
    xn = (x * lax.rsqrt(ms + EPS) * gin_ref[...]).astype(BF16)

    yg = _dot(xn, wg_ref[...])
    oq_ref[...] = yg[:, :GLA_QK_W] * (GLA_DK ** -0.5)
    ok_ref[...] = yg[:, GLA_QK_W:2 * GLA_QK_W]
    ov_ref[...] = yg[:, 2 * GLA_QK_W:2 * GLA_QK_W + GLA_V_W]
    or_ref[...] = yg[:, 2 * GLA_QK_W + GLA_V_W:]

    lr = _dot(xn, wlr_ref[...])
    lr_hi, lr_lo = _split2(lr)
    wu_hi = wup_ref[0]
    wu_lo = wup_ref[1]
    z = _dot(lr_hi, wu_hi) + _dot(lr_lo, wu_hi) + _dot(lr_hi, wu_lo) + bgate_ref[...]
    ola_ref[...] = (jnp.minimum(z, 0.0) - jnp.log(1.0 + jnp.exp(-jnp.abs(z)))) * (1.0 / GLA_TAU)

    ym = _dot(xn, wm_ref[...])
    reps = MOBA_W // LANES
    cos = jnp.concatenate([cos_ref[...]] * reps, axis=1)
    sin = jnp.concatenate([sin_ref[...]] * reps, axis=1)
    mq = _rope(_head_norm(ym[:, :MOBA_W], gq_ref[...]), cos, sin)
    mk = _rope(_head_norm(ym[:, MOBA_W:2 * MOBA_W], gk_ref[...]), cos, sin)
    mv = ym[:, 2 * MOBA_W:]
    if transposed:
        qt_ref, kt_ref, vt_ref, kn_ref = moba_refs
        qt_ref[...] = jnp.transpose(mq * (MOBA_HD ** -0.5 * LOG2E)).astype(BF16)
        kt_ref[...] = jnp.transpose(mk)
        vt_ref[...] = jnp.transpose(mv)
        kn_ref[...] = mk.astype(BF16)
    else:
        omq_ref, omk_ref, omv_ref = moba_refs
        omq_ref[...] = mq
        omk_ref[...] = mk
        omv_ref[...] = mv


def _inproj(x2d, w, cos_tab, sin_tab, seq_len=None):
    n, d = x2d.shape
    t = ROW_TILE
    assert n % t == 0 and cos_tab.shape[0] % t == 0
    n_pos_tiles = cos_tab.shape[0] // t
    row = lambda i: (i, 0)
    fixed = lambda i: (0, 0)
    widths = (GLA_QK_W, GLA_QK_W, GLA_V_W, GLA_V_W, GLA_QK_W)
    out_specs = [pl.BlockSpec((t, wd), row) for wd in widths]
    out_shape = [jax.ShapeDtypeStruct((n, wd), F32) for wd in widths]
    if seq_len is None:
        out_specs += [pl.BlockSpec((t, MOBA_W), row)] * 3
        out_shape += [jax.ShapeDtypeStruct((n, MOBA_W), F32)] * 3
    else:
        assert seq_len % t == 0
        tiles = seq_len // t
        bsz = n // seq_len
        tok_lanes = lambda i: (i // tiles, 0, i % tiles)
        out_specs += [pl.BlockSpec((None, MOBA_W, t), tok_lanes)] * 3
        out_specs += [pl.BlockSpec((None, t, MOBA_W), lambda i: (i // tiles, i % tiles, 0))]
        out_shape += [jax.ShapeDtypeStruct((bsz, MOBA_W, seq_len), dt) for dt in (BF16, F32, F32)]
        out_shape += [jax.ShapeDtypeStruct((bsz, seq_len, MOBA_W), BF16)]
    return pl.pallas_call(
        functools.partial(_inproj_kernel, transposed=seq_len is not None),
        grid=(n // t,),
        in_specs=[
            pl.BlockSpec((t, d), row),
            pl.BlockSpec((1, d), fixed),
            pl.BlockSpec(w["wg"].shape, fixed),
            pl.BlockSpec(w["wlr"].shape, fixed),
            pl.BlockSpec(w["wup"].shape, lambda i: (0, 0, 0)),
            pl.BlockSpec((1, GLA_QK_W), fixed),
            pl.BlockSpec(w["wm"].shape, fixed),
            pl.BlockSpec((1, MOBA_W), fixed),
            pl.BlockSpec((1, MOBA_W), fixed),
            pl.BlockSpec((t, LANES), lambda i: (i % n_pos_tiles, 0)),
            pl.BlockSpec((t, LANES), lambda i: (i % n_pos_tiles, 0)),
        ],
        out_specs=out_specs,
        out_shape=out_shape,
        compiler_params=_params(1),
        name="inproj",
    )(x2d, w["g_attn"], w["wg"], w["wlr"], w["wup"], w["b_gate"], w["wm"], w["g_q"], w["g_k"], cos_tab, sin_tab)


def _rope_tables(positions):
    half = MOBA_HD // 2
    inv = ROPE_THETA ** (-np.arange(half, dtype=np.float64) / half)
    ang = positions.astype(np.float64)[:, None] * inv[None, :]
    cos = np.cos(ang)
    sin = np.sin(ang)
    cos_h = np.concatenate([cos, cos], axis=1)
    sin_h = np.concatenate([-sin, sin], axis=1)
    reps = LANES // MOBA_HD
    return (jnp.asarray(np.tile(cos_h, (1, reps)), F32), jnp.asarray(np.tile(sin_h, (1, reps)), F32))


def _gla_level_matrix(c):
    t = np.arange(c)[:, None]
    j = np.arange(c)[None, :]
    mats = [(j <= t)]
    m = c // 2
    while m >= 1:
        base = (t // (2 * m)) * (2 * m)
        ref = base + m - 1
        upper = (t - base) >= m
        mats.append(np.where(upper, (j > ref) & (j <= t), (j > t) & (j <= ref)))
        m //= 2
    return jnp.asarray(np.concatenate(mats, axis=0).astype(np.float32), BF16)


def _gla_kernel(q_ref, k_ref, v_ref, la_ref, s0_ref, lvl_ref, o_ref, sout_ref, s_ref, *, c):
    ci = pl.program_id(1)
    n_lev = int(math.log2(c))
    pair_w = 2 * GLA_DK

    @pl.when(ci == 0)
    def _():
        s_ref[...] = s0_ref[...]

    la = la_ref[...]
    la_hi, la_lo = _split2(la)
    lvl = lvl_ref[...]
    x_all = _dot(lvl, la_hi) + _dot(lvl, la_lo)
    b = x_all[0:c]
    b_last = b[c - 1:c, :]
    q = q_ref[...]
    k = k_ref[...]
    qe = q * jnp.exp(b)
    kd = (k * jnp.exp(b_last - b)).astype(BF16)

    ti = lax.broadcasted_iota(jnp.int32, (c, c), 0)
    si = lax.broadcasted_iota(jnp.int32, (c, c), 1)
    trow = lax.broadcasted_iota(jnp.int32, (c, 1), 0)
    lane = lax.broadcasted_iota(jnp.int32, (1, pair_w), 1)
    ones_cw = jnp.ones((c, pair_w), BF16)
    srow = lax.broadcasted_iota(jnp.int32, (pair_w, 1), 0)

    for p in range(GLA_HEADS // 2):
        cols = slice(p * pair_w, (p + 1) * pair_w)
        q_p = q[:, cols]
        k_p = k[:, cols]
        qe_p = qe[:, cols]
        head_lanes = (lane < GLA_DK, lane >= GLA_DK)
        s_pair = s_ref[cols, :]
        s_pair_b = s_pair.astype(BF16)

        a = [jnp.where(ti == si, _dot_nt(jnp.where(hm, q_p, 0.0).astype(BF16), k_p.astype(BF16)), 0.0)
             for hm in head_lanes]
        for lev in range(n_lev):
            m_log = n_lev - 1 - lev
            e = jnp.exp(x_all[(lev + 1) * c:(lev + 2) * c, cols])
            upper = ((trow >> m_log) & 1) == 1
            kt = jnp.where(upper, 0.0, k_p * e).astype(BF16)
            qt = jnp.where(upper, q_p * e, 0.0)
            same = (ti >> (m_log + 1)) == (si >> (m_log + 1))
            for hh in range(2):
                al = _dot_nt(jnp.where(head_lanes[hh], qt, 0.0).astype(BF16), kt)
                a[hh] = a[hh] + jnp.where(same, al, 0.0)

        u = []
        for hh in range(2):
            h = 2 * p + hh
            v_h = v_ref[:, h * GLA_DV:(h + 1) * GLA_DV].astype(BF16)
            o_h = _dot(a[hh].astype(BF16), v_h) + _dot(jnp.where(head_lanes[hh], qe_p, 0.0).astype(BF16), s_pair_b)
            o_ref[:, h * GLA_DV:(h + 1) * GLA_DV] = o_h
            u.append(_dot_tn(kd[:, cols], v_h))
        bl_rows = _dot_tn(la_hi[:, cols], ones_cw) + _dot_tn(la_lo[:, cols], ones_cw)
        s_ref[cols, :] = jnp.exp(bl_rows[:, :GLA_DV]) * s_pair + jnp.where(srow < GLA_DK, u[0], u[1])

    @pl.when(ci == pl.num_programs(1) - 1)
    def _():
        sout_ref[...] = s_ref[...]


def _gla(q, k, v, la, s0, c):
    bsz, l, _ = q.shape
    assert l % c == 0 and (c & (c - 1)) == 0 and GLA_DV == 2 * GLA_DK
    lvl = _gla_level_matrix(c)
    tok = lambda b, i: (b, i, 0)
    per_b = lambda b, i: (b, 0, 0)
    return pl.pallas_call(
        functools.partial(_gla_kernel, c=c),
        grid=(bsz, l // c),
        in_specs=[
            pl.BlockSpec((None, c, GLA_QK_W), tok),
            pl.BlockSpec((None, c, GLA_QK_W), tok),
            pl.BlockSpec((None, c, GLA_V_W), tok),
            pl.BlockSpec((None, c, GLA_QK_W), tok),
            pl.BlockSpec((None, GLA_QK_W, GLA_DV), per_b),
            pl.BlockSpec(lvl.shape, lambda b, i: (0, 0)),
        ],
        out_specs=[pl.BlockSpec((None, c, GLA_V_W), tok), pl.BlockSpec((None, GLA_QK_W, GLA_DV), per_b)],
        out_shape=[jax.ShapeDtypeStruct((bsz, l, GLA_V_W), F32), jax.ShapeDtypeStruct((bsz, GLA_QK_W, GLA_DV), F32)],
        scratch_shapes=[pltpu.VMEM((GLA_QK_W, GLA_DV), F32)],
        compiler_params=_params(2),
        name="gla",
    )(q, k, v, la, s0, lvl)


def _select_topk_blocks(gate, n_valid):
    row = lax.broadcasted_iota(jnp.int32, gate.shape, 0)
    g = jnp.where(row < n_valid, gate, NEG)
    sel = jnp.zeros(gate.shape, F32)
    for r in range(MOBA_TOPK):
        mx = jnp.max(g, axis=0, keepdims=True)
        idx = jnp.min(jnp.where(g == mx, row, gate.shape[0]), axis=0, keepdims=True)
        pick = row == jnp.where(r < n_valid, idx, -1)
        sel = jnp.where(pick, 1.0, sel)
        g = jnp.where(pick, NEG, g)
    return sel


def _moba_prompt_kernel(q_ref, k_ref, v_ref, o_ref, kmean_ref, sel_ref, sa_ref, sb_ref, *, nb):
    qi = pl.program_id(2)
    blk = MOBA_BLOCK
    hd = MOBA_HD
    heads = range(2)

    def blk_ds(n):
        return pl.ds(pl.multiple_of(n * blk, blk), blk)

    @pl.when(qi == 0)
    def _():
        kmean_ref[...] = jnp.zeros(kmean_ref.shape, F32)

        def body(n, carry):
            kb = k_ref[blk_ds(n), :].astype(F32)
            kmean_ref[pl.ds(n, 1), :] = jnp.sum(kb, axis=0, keepdims=True) * (1.0 / blk)
            return carry

        lax.fori_loop(0, nb, body, 0)

    row = lax.broadcasted_iota(jnp.int32, (2 * hd, blk), 0)
    q_both = q_ref[...]
    q_pair = [jnp.where((row // hd) == hh, q_both, jnp.zeros((), BF16)) for hh in heads]
    km_hi, km_lo = _split2(kmean_ref[...])
    for hh in heads:
        gate = _dot(km_hi, q_pair[hh]) + _dot(km_lo, q_pair[hh])
        sel_ref[hh] = _select_topk_blocks(gate, qi)

    ones_rows = jnp.where(lax.broadcasted_iota(jnp.int32, (16, blk), 0) == 0, 1.0, 0.0).astype(BF16)

    def v_aug(hh, n):
        return jnp.concatenate([v_ref[hh * hd:(hh + 1) * hd, blk_ds(n)].astype(BF16), ones_rows], axis=0)

    def update(carry, scores, v_tiles, picked):
        m, acc = carry
        m_new = m
        for s, pk in zip(scores, picked):
            bm = jnp.max(s, axis=0, keepdims=True)
            m_new = jnp.maximum(m_new, bm if pk is None else jnp.where(pk, bm, NEG))
        acc = acc * jnp.exp2(m - m_new)
        for s, v_t, pk in zip(scores, v_tiles, picked):
            c = _dot(v_t, jnp.exp2(s - m_new).astype(BF16))
            acc = acc + (c if pk is None else jnp.where(pk, c, 0.0))
        return m_new, acc

    half = MOBA_GROUP // 2

    def clamp(n):
        return jnp.minimum(n, nb - 1)

    def stage_scores(dst_ref, first):
        for hh in heads:
            for g in range(half):
                dst_ref[hh * half + g] = _dot(k_ref[blk_ds(clamp(first + g)), :], q_pair[hh])

    def stage_update(src_ref, first, carries):
        out = []
        for hh in heads:
            blocks = [clamp(first + g) for g in range(half)]
            scores = [src_ref[hh * half + g] for g in range(half)]
            picked = [sel_ref[hh, pl.ds(n, 1), :] > 0.5 for n in blocks]
            out.append(update(carries[hh], scores, [v_aug(hh, n) for n in blocks], picked))
        return tuple(out)

    def past_group(gi, carries):
        base = gi * MOBA_GROUP
        stage_scores(sb_ref, base + half)
        carries = stage_update(sa_ref, base, carries)
        stage_scores(sa_ref, base + MOBA_GROUP)
        return stage_update(sb_ref, base + half, carries)

    causal = lax.broadcasted_iota(jnp.int32, (blk, blk), 0) <= lax.broadcasted_iota(jnp.int32, (blk, blk), 1)
    k_own = k_ref[blk_ds(qi), :]
    init = (jnp.full((1, blk), NEG, F32), jnp.zeros((hd + 16, blk), F32))
    carries = tuple(update(init, [jnp.where(causal, _dot(k_own, q_pair[hh]), NEG)], [v_aug(hh, qi)], [None])
                    for hh in heads)
    stage_scores(sa_ref, 0)
    carries = lax.fori_loop(0, (qi + MOBA_GROUP - 1) // MOBA_GROUP, past_group, carries)
    o_ref[...] = jnp.concatenate([jnp.transpose(acc[:hd] / acc[hd:hd + 1]) for _, acc in carries], axis=1)


def _moba_prompt(qt, k_nat, vt):
    bsz, _, l = qt.shape
    blk = MOBA_BLOCK
    pair_w = 2 * MOBA_HD
    assert l % blk == 0 and pair_w == LANES
    nb = l // blk
    nb_pad = -(-nb // SUBLANES) * SUBLANES
    return pl.pallas_call(
        functools.partial(_moba_prompt_kernel, nb=nb),
        grid=(bsz, MOBA_HEADS // 2, nb),
        in_specs=[
            pl.BlockSpec((None, pair_w, blk), lambda b, p, i: (b, p, i)),
            pl.BlockSpec((None, l, pair_w), lambda b, p, i: (b, 0, p)),
            pl.BlockSpec((None, pair_w, l), lambda b, p, i: (b, p, 0)),
        ],
        out_specs=pl.BlockSpec((None, blk, pair_w), lambda b, p, i: (b, i, p)),
        out_shape=jax.ShapeDtypeStruct((bsz, l, MOBA_W), F32),
        scratch_shapes=[pltpu.VMEM((nb_pad, pair_w), F32), pltpu.VMEM((2, nb_pad, blk), F32),
                        pltpu.VMEM((MOBA_GROUP, blk, blk), F32), pltpu.VMEM((MOBA_GROUP, blk, blk), F32)],
        compiler_params=_params(3),
        name="moba_prompt",
    )(qt, k_nat, vt)


def _select_topk_lanes(gate, n_valid):
    lane = lax.broadcasted_iota(jnp.int32, gate.shape, 1)
    g = jnp.where(lane < n_valid, gate, NEG)
    sel = jnp.zeros(gate.shape, F32)
    for r in range(min(MOBA_TOPK, n_valid)):
        mx = jnp.max(g, axis=1, keepdims=True)
        idx = jnp.min(jnp.where(g == mx, lane, gate.shape[1]), axis=1, keepdims=True)
        pick = lane == idx
        sel = jnp.where(pick, 1.0, sel)
        g = jnp.where(pick, NEG, g)
    return sel


def _flat_page(ref):
    return ref[...].reshape(MOBA_W, ref.shape[-1])


def _moba_sample_probs_kernel(pt_ref, qb_ref, kn_ref, *refs, npp, n_steps, n_pages, page, t_new):
    k_refs = refs[:npp]
    p_ref, s_ref = refs[npp:]
    j = pl.program_id(1)
    blk = MOBA_BLOCK
    n_blk = (n_pages * page) // blk
    past = n_pages * page
    n_c = qb_ref.shape[0]
    t_pad = n_c // MOBA_HEADS

    qb = qb_ref[...]
    for i in range(0, npp, 2):
        w = jnp.concatenate([_flat_page(k_refs[i]), _flat_page(k_refs[i + 1])], axis=1).astype(BF16)
        cols = pl.ds(pl.multiple_of((j * npp + i) * page, 2 * page), 2 * page)
        s_ref[:, cols] = _dot(qb, w)

    @pl.when(j == n_steps - 1)
    def _():
        lane = lax.broadcasted_iota(jnp.int32, (n_c, LANES), 1)

        def blk_cols(n):
            return pl.ds(pl.multiple_of(n * blk, blk), blk)

        def gate_body(n, gate):
            gsum = jnp.sum(s_ref[:, blk_cols(n)], axis=1, keepdims=True) * (1.0 / blk)
            return jnp.where(lane == n, gsum, gate)

        unroll = math.gcd(n_blk, 8)
        gate = lax.fori_loop(0, n_blk, gate_body, jnp.zeros((n_c, LANES), F32), unroll=unroll)
        sel = _select_topk_lanes(gate, n_blk)

        s_own = _dot(qb, kn_ref[...].astype(BF16))
        key_t = lax.broadcasted_iota(jnp.int32, (n_c, page), 1)
        row_t = lax.broadcasted_iota(jnp.int32, (n_c, page), 0) % t_pad
        s_own = jnp.where((key_t < t_new) & (key_t <= row_t), s_own, NEG)

        def masked(n):
            picked = jnp.sum(jnp.where(lane == n, sel, 0.0), axis=1, keepdims=True) > 0.5
            return jnp.where(picked, s_ref[:, blk_cols(n)], NEG)

        def max_body(n, m):
            return jnp.maximum(m, masked(n))

        m_run = lax.fori_loop(0, n_blk, max_body, jnp.full((n_c, blk), NEG, F32), unroll=unroll)
        m = jnp.maximum(jnp.max(m_run, axis=1, keepdims=True), jnp.max(s_own, axis=1, keepdims=True))

        def exp_body(n, l_run):
            e = jnp.exp(masked(n) - m)
            s_ref[:, blk_cols(n)] = e
            return l_run + e

        l_run = lax.fori_loop(0, n_blk, exp_body, jnp.zeros((n_c, blk), F32), unroll=unroll)
        e_own = jnp.exp(s_own - m)
        inv = 1.0 / (jnp.sum(l_run, axis=1, keepdims=True) + jnp.sum(e_own, axis=1, keepdims=True))

        def norm_body(n, carry):
            p_ref[:, blk_cols(n)] = (s_ref[:, blk_cols(n)] * inv).astype(BF16)
            return carry

        lax.fori_loop(0, n_blk, norm_body, 0, unroll=unroll)
        p_ref[:, past:past + page] = (e_own * inv).astype(BF16)


def _moba_sample_values_kernel(pt_ref, p_ref, vn_ref, *refs, npp, n_steps, n_pages, page):
    v_refs = refs[:npp]
    o_ref, acc_ref = refs[npp:]
    j = pl.program_id(1)
    past = n_pages * page

    @pl.when(j == 0)
    def _():
        acc_ref[...] = jnp.zeros(acc_ref.shape, F32)

    v_cat = jnp.concatenate([_flat_page(v_refs[i]) for i in range(npp)], axis=1).astype(BF16)
    cols = pl.ds(pl.multiple_of(j * (npp * page), npp * page), npp * page)
    acc_ref[...] += _dot_nt(v_cat, p_ref[:, cols])

    @pl.when(j == n_steps - 1)
    def _():
        o_ref[...] = acc_ref[...] + _dot_nt(vn_ref[...].astype(BF16), p_ref[:, past:past + page])


def _moba_sample(mq, mk, mv, cache_kt, cache_vt, layer, page_table):
    dbs, t_new, _ = mq.shape
    page = cache_kt.shape[-1]
    n_pages = page_table.shape[1]
    npp = SAMPLE_PAGES_PER_STEP
    t_pad = t_new + t_new % 2
    n_c = MOBA_HEADS * t_pad
    assert n_pages % npp == 0 and npp % 2 == 0 and page == LANES
    assert (n_pages * page) % MOBA_BLOCK == 0 and (n_pages * page) // MOBA_BLOCK <= LANES and t_new <= page
    n_steps = n_pages // npp

    q4 = (mq * (MOBA_HD ** -0.5)).reshape(dbs, t_new, MOBA_HEADS, MOBA_HD)
    q4 = jnp.pad(q4, ((0, 0), (0, t_pad - t_new), (0, 0), (0, 0)))
    eye = jnp.eye(MOBA_HEADS, dtype=F32)
    qb = (q4[:, :, :, None, :] * eye[None, None, :, :, None]).transpose(0, 2, 1, 3, 4)
    qb = qb.reshape(dbs, n_c, MOBA_W).astype(BF16)
    kn = jnp.pad(jnp.transpose(mk, (0, 2, 1)), ((0, 0), (0, 0), (0, page - t_new)))
    vn = jnp.pad(jnp.transpose(mv, (0, 2, 1)), ((0, 0), (0, 0), (0, page - t_new)))
    pt_flat = page_table.reshape(-1).astype(jnp.int32)

    def page_map(i):
        return lambda b, j, pt: (layer, pt[b * n_pages + j * npp + i], 0, 0, 0)

    per_b = lambda b, j, pt: (b, 0, 0)
    page_specs = [pl.BlockSpec((None, None, MOBA_HEADS, MOBA_HD, page), page_map(i)) for i in range(npp)]
    n_keys = (n_pages + 1) * page
    statics = dict(npp=npp, n_steps=n_steps, n_pages=n_pages, page=page)
    probs = pl.pallas_call(
        functools.partial(_moba_sample_probs_kernel, t_new=t_new, **statics),
        grid_spec=pltpu.PrefetchScalarGridSpec(
            num_scalar_prefetch=1,
            grid=(dbs, n_steps),
            in_specs=[pl.BlockSpec((None, n_c, MOBA_W), per_b), pl.BlockSpec((None, MOBA_W, page), per_b)] + page_specs,
            out_specs=pl.BlockSpec((None, n_c, n_keys), per_b),
            scratch_shapes=[pltpu.VMEM((n_c, n_pages * page), F32)],
        ),
        out_shape=jax.ShapeDtypeStruct((dbs, n_c, n_keys), BF16),
        compiler_params=_params(2),
        name="moba_sample_probs",
    )(pt_flat, qb, kn, *([cache_kt] * npp))
    out_t = pl.pallas_call(
        functools.partial(_moba_sample_values_kernel, **statics),
        grid_spec=pltpu.PrefetchScalarGridSpec(
            num_scalar_prefetch=1,
            grid=(dbs, n_steps),
            in_specs=[pl.BlockSpec((None, n_c, n_keys), per_b), pl.BlockSpec((None, MOBA_W, page), per_b)] + page_specs,
            out_specs=pl.BlockSpec((None, MOBA_W, n_c), per_b),
            scratch_shapes=[pltpu.VMEM((MOBA_W, n_c), F32)],
        ),
        out_shape=jax.ShapeDtypeStruct((dbs, MOBA_W, n_c), F32),
        compiler_params=_params(2),
        name="moba_sample_values",
    )(pt_flat, probs, vn, *([cache_vt] * npp))
    o5 = out_t.reshape(dbs, MOBA_HEADS, MOBA_HD, MOBA_HEADS, t_pad)
    o = jnp.sum(o5 * eye[None, :, None, :, None], axis=3)
    return o.transpose(0, 3, 1, 2)[:, :t_new].reshape(dbs, t_new, MOBA_W)


def _outproj_kernel(x_ref, go_ref, gr_ref, mo_ref, ggo_ref, wo1_ref, wo2_ref, gffn_ref, wr_ref, br_ref,
                    h_ref, xn_ref, eid_ref, wt_ref, rank_ref, cnt_ref, carry_ref):
    i = pl.program_id(0)
    t = x_ref.shape[0]

    @pl.when(i == 0)
    def _():
        carry_ref[...] = jnp.zeros(carry_ref.shape, F32)

    gr = gr_ref[...]
    gate = gr / (1.0 + jnp.exp(-gr))
    parts = []
    for hd in range(GLA_HEADS):
        cols = slice(hd * GLA_DV, (hd + 1) * GLA_DV)
        o = go_ref[:, cols]
        ms = jnp.mean(o * o, axis=-1, keepdims=True)
        parts.append(o * lax.rsqrt(ms + EPS) * ggo_ref[...] * gate[:, cols])
    a1 = jnp.concatenate(parts, axis=1).astype(BF16)
    h = x_ref[...] + _dot(a1, wo1_ref[...]) + _dot(mo_ref[...].astype(BF16), wo2_ref[...])
    h_ref[...] = h

    ms = jnp.mean(h * h, axis=-1, keepdims=True)
    xn = h * lax.rsqrt(ms + EPS) * gffn_ref[...]
    xn_ref[...] = xn
    logits = _dot_nt(wr_ref[...], xn.astype(BF16)) + br_ref[...]

    gl = logits[0:N_GROUPS]
    grow = lax.broadcasted_iota(jnp.int32, gl.shape, 0)
    gmax = jnp.max(gl, axis=0, keepdims=True)
    g_sel = jnp.min(jnp.where(gl == gmax, grow, N_GROUPS), axis=0, keepdims=True)
    gw = 1.0 / jnp.sum(jnp.exp(gl - gmax), axis=0, keepdims=True)

    el = logits[SUBLANES:SUBLANES + N_EXPERTS]
    erow = lax.broadcasted_iota(jnp.int32, el.shape, 0)
    in_grp = (erow // EXPERTS_PER_GROUP) == g_sel
    emax = jnp.max(jnp.where(in_grp, el, NEG), axis=0, keepdims=True)
    ee = jnp.where(in_grp, jnp.exp(el - emax), 0.0)
    ep = ee / jnp.sum(ee, axis=0, keepdims=True)
    cand = jnp.where(in_grp, ep, -1.0)
    p1 = jnp.max(cand, axis=0, keepdims=True)
    i1 = jnp.min(jnp.where(cand == p1, erow, N_EXPERTS), axis=0, keepdims=True)
    cand = jnp.where(erow == i1, -1.0, cand)
    p2 = jnp.max(cand, axis=0, keepdims=True)
    i2 = jnp.min(jnp.where(cand == p2, erow, N_EXPERTS), axis=0, keepdims=True)
    denom = p1 + p2
    zero_i = jnp.zeros((SUBLANES - EXPERT_TOPK, t), jnp.int32)
    zero_f = jnp.zeros((SUBLANES - EXPERT_TOPK, t), F32)
    eid_ref[...] = jnp.concatenate([i1, i2, zero_i], axis=0)
    wt_ref[...] = jnp.concatenate([p1 / denom * gw, p2 / denom * gw, zero_f], axis=0)

    oh1 = erow == i1
    oh2 = erow == i2
    hits = jnp.where(oh1 | oh2, 1.0, 0.0)
    before = (lax.broadcasted_iota(jnp.int32, (t, t), 0) < lax.broadcasted_iota(jnp.int32, (t, t), 1))
    prefix = _dot(hits.astype(BF16), jnp.where(before, 1.0, 0.0).astype(BF16)) + carry_ref[...]
    r1 = jnp.sum(jnp.where(oh1, prefix, 0.0), axis=0, keepdims=True)
    r2 = jnp.sum(jnp.where(oh2, prefix, 0.0), axis=0, keepdims=True)
    rank_ref[...] = jnp.concatenate([r1.astype(jnp.int32), r2.astype(jnp.int32), zero_i], axis=0)
    carry_ref[...] += jnp.sum(hits, axis=1, keepdims=True)
    cnt_ref[...] = jnp.broadcast_to(carry_ref[...], cnt_ref.shape)


def _outproj(x2d, go, gr, mo, w):
    n, d = x2d.shape
    t = ROW_TILE
    assert n % t == 0
    row = lambda i: (i, 0)
    col = lambda i: (0, i)
    fixed = lambda i: (0, 0)
    return pl.pallas_call(
        _outproj_kernel,
        grid=(n // t,),
        in_specs=[
            pl.BlockSpec((t, d), row),
            pl.BlockSpec((t, GLA_V_W), row),
            pl.BlockSpec((t, GLA_V_W), row),
            pl.BlockSpec((t, MOBA_W), row),
            pl.BlockSpec((1, GLA_DV), fixed),
            pl.BlockSpec(w["wo1"].shape, fixed),
            pl.BlockSpec(w["wo2"].shape, fixed),
            pl.BlockSpec((1, d), fixed),
            pl.BlockSpec(w["wr"].shape, fixed),
            pl.BlockSpec(w["br"].shape, fixed),
        ],
        out_specs=[pl.BlockSpec((t, d), row), pl.BlockSpec((t, d), row),
                   pl.BlockSpec((SUBLANES, t), col), pl.BlockSpec((SUBLANES, t), col),
                   pl.BlockSpec((SUBLANES, t), col), pl.BlockSpec((N_EXPERTS, LANES), fixed)],
        out_shape=[jax.ShapeDtypeStruct((n, d), F32), jax.ShapeDtypeStruct((n, d), F32),
                   jax.ShapeDtypeStruct((SUBLANES, n), jnp.int32), jax.ShapeDtypeStruct((SUBLANES, n), F32),
                   jax.ShapeDtypeStruct((SUBLANES, n), jnp.int32), jax.ShapeDtypeStruct((N_EXPERTS, LANES), F32)],
        scratch_shapes=[pltpu.VMEM((N_EXPERTS, 1), F32)],
        compiler_params=_params(1),
        name="outproj",
    )(x2d, go, gr, mo, w["g_gla_out"], w["wo1"], w["wo2"], w["g_ffn"], w["wr"], w["br"])


def _row_copy(src_ref, src_row, dst_ref, dst_row, sem):
    return pltpu.make_async_copy(src_ref.at[pl.ds(src_row, 1)], dst_ref.at[pl.ds(dst_row, 1)], sem)


def _moe_scatter_kernel(dest_ref, x_ref, xs_in_ref, xs_ref, sem):
    del xs_in_ref
    t = x_ref.shape[0]

    def start(r, carry):
        for kk in range(EXPERT_TOPK):
            _row_copy(x_ref, r, xs_ref, dest_ref[kk, r], sem).start(priority=kk % 2)
        return carry

    def wait(r, carry):
        for kk in range(EXPERT_TOPK):
            _row_copy(x_ref, 0, xs_ref, 0, sem).wait()
        return carry

    lax.fori_loop(0, t, start, 0, unroll=ROW_DMA_UNROLL)
    lax.fori_loop(0, t, wait, 0, unroll=ROW_DMA_UNROLL)


def _moe_scatter(xn, dest, n_rows):
    n, d = xn.shape
    t = ROW_TILE
    xs0 = jnp.zeros((n_rows, d), F32)
    return pl.pallas_call(
        _moe_scatter_kernel,
        grid=(n // t,),
        in_specs=[
            pl.BlockSpec((EXPERT_TOPK, t), lambda i: (0, i), memory_space=pltpu.SMEM),
            pl.BlockSpec((t, d), lambda i: (i, 0)),
            pl.BlockSpec(memory_space=pl.ANY),
        ],
        out_specs=pl.BlockSpec(memory_space=pl.ANY),
        out_shape=jax.ShapeDtypeStruct((n_rows, d), F32),
        scratch_shapes=[pltpu.SemaphoreType.DMA(())],
        input_output_aliases={2: 0},
        compiler_params=_params(1),
        name="moe_scatter",
    )(dest, xn, xs0)


def _moe_ffn_kernel(be_ref, nu_ref, xs_ref, wg_ref, wu_ref, wd_ref, ys_ref):
    i = pl.program_id(0)

    @pl.when(i < nu_ref[0])
    def _():
        x = xs_ref[...].astype(BF16)
        g = _dot(x, wg_ref[...])
        u = _dot(x, wu_ref[...])
        hmid = (g / (1.0 + jnp.exp(-g)) * u).astype(BF16)
        ys_ref[...] = _dot(hmid, wd_ref[...])

    @pl.when(i >= nu_ref[0])
    def _():
        ys_ref[...] = jnp.zeros(ys_ref.shape, F32)


def _moe_ffn(xs, blk_expert, n_used, wg, wu, wd):
    r, d = xs.shape
    de = wg.shape[2]
    n_blk = r // MOE_BLOCK
    rows = lambda i, be, nu: (jnp.minimum(i, nu[0] - 1), 0)
    wsel = lambda i, be, nu: (be[jnp.minimum(i, nu[0] - 1)], 0, 0)
    grid_spec = pltpu.PrefetchScalarGridSpec(
        num_scalar_prefetch=2,
        grid=(n_blk,),
        in_specs=[
            pl.BlockSpec((MOE_BLOCK, d), rows),
            pl.BlockSpec((None, d, de), wsel),
            pl.BlockSpec((None, d, de), wsel),
            pl.BlockSpec((None, de, d), wsel),
        ],
        out_specs=pl.BlockSpec((MOE_BLOCK, d), lambda i, be, nu: (i, 0)),
    )
    return pl.pallas_call(
        _moe_ffn_kernel,
        grid_spec=grid_spec,
        out_shape=jax.ShapeDtypeStruct((r, d), F32),
        compiler_params=_params(1),
        name="moe_ffn",
    )(blk_expert, n_used, xs, wg, wu, wd)


def _moe_combine_kernel(dest_ref, h_ref, wt_ref, ys_ref, y_ref, buf_ref, sem):
    t = h_ref.shape[0]

    def start(r, carry):
        for kk in range(EXPERT_TOPK):
            _row_copy(ys_ref, dest_ref[kk, r], buf_ref.at[kk], r, sem).start(priority=kk % 2)
        return carry

    def wait(r, carry):
        for kk in range(EXPERT_TOPK):
            _row_copy(ys_ref, 0, buf_ref.at[kk], 0, sem).wait()
        return carry

    lax.fori_loop(0, t, start, 0, unroll=ROW_DMA_UNROLL)
    lax.fori_loop(0, t, wait, 0, unroll=ROW_DMA_UNROLL)
    y = h_ref[...]
    for kk in range(EXPERT_TOPK):
        y = y + buf_ref[kk] * wt_ref[:, kk:kk + 1]
    y_ref[...] = y


def _moe_combine(h, wt_rows, dest, ys):
    n, d = h.shape
    t = ROW_TILE
    return pl.pallas_call(
        _moe_combine_kernel,
        grid=(n // t,),
        in_specs=[
            pl.BlockSpec((EXPERT_TOPK, t), lambda i: (0, i), memory_space=pltpu.SMEM),
            pl.BlockSpec((t, d), lambda i: (i, 0)),
            pl.BlockSpec((t, SUBLANES), lambda i: (i, 0)),
            pl.BlockSpec(memory_space=pl.ANY),
        ],
        out_specs=pl.BlockSpec((t, d), lambda i: (i, 0)),
        out_shape=jax.ShapeDtypeStruct((n, d), F32),
        scratch_shapes=[pltpu.VMEM((EXPERT_TOPK, t, d), F32), pltpu.SemaphoreType.DMA(())],
        compiler_params=_params(1),
        name="moe_combine",
    )(dest, h, wt_rows, ys)


def _moe(h, xn, eid, wt, rank, counts, w):
    n, d = h.shape
    blk = MOE_BLOCK
    n_blk = -(-(n * EXPERT_TOPK) // blk) + N_EXPERTS
    cnt = counts[:, 0].astype(jnp.int32)
    padded = (cnt + blk - 1) // blk * blk
    pend = jnp.cumsum(padded)
    pstart = pend - padded
    dest = rank[:EXPERT_TOPK]
    for e in range(N_EXPERTS):
        dest = dest + jnp.where(eid[:EXPERT_TOPK] == e, pstart[e], 0)
    blk_expert = jnp.minimum(jnp.searchsorted(pend, jnp.arange(n_blk) * blk, side="right"),
                             N_EXPERTS - 1).astype(jnp.int32)
    n_used = (pend[-1:] // blk).astype(jnp.int32)

    xs = _moe_scatter(xn, dest, n_blk * blk)
    ys = _moe_ffn(xs, blk_expert, n_used, w["w_gate"], w["w_up"], w["w_down"])
    return _moe_combine(h, jnp.transpose(wt), dest, ys)


def _prep_weights(g_attn_norm, w_in, w_gla_gate_up, b_gla_gate, g_gla_out, g_q, g_k, w_out, g_ffn_norm,
                  w_group_router, b_group_router, w_expert_router, b_expert_router, w_gate, w_up, w_down):
    d = w_in.shape[0]
    o = np.cumsum((GLA_QK_W, GLA_QK_W, GLA_V_W, GLA_V_W, GLA_GATE_RANK, MOBA_W, MOBA_W, MOBA_W))
    wlr = jnp.pad(w_in[:, o[3]:o[4]], ((0, 0), (0, LANES - GLA_GATE_RANK)))
    wup = jnp.pad(w_gla_gate_up, ((0, LANES - GLA_GATE_RANK), (0, 0)))
    wup_hi = wup.astype(BF16)
    wup_lo = (wup - wup_hi.astype(F32)).astype(BF16)
    wr = jnp.zeros((4 * SUBLANES, d), F32)
    wr = wr.at[0:N_GROUPS].set(w_group_router.T).at[SUBLANES:SUBLANES + N_EXPERTS].set(w_expert_router.T)
    br = jnp.zeros((4 * SUBLANES, 1), F32)
    br = br.at[0:N_GROUPS, 0].set(b_group_router).at[SUBLANES:SUBLANES + N_EXPERTS, 0].set(b_expert_router)
    return {
        "g_attn": g_attn_norm[None, :],
        "wg": w_in[:, :o[3]].astype(BF16),
        "wlr": wlr.astype(BF16),
        "wup": jnp.stack([wup_hi, wup_lo]),
        "b_gate": b_gla_gate[None, :],
        "wm": w_in[:, o[4]:].astype(BF16),
        "g_q": jnp.tile(g_q, MOBA_HEADS)[None, :],
        "g_k": jnp.tile(g_k, MOBA_HEADS)[None, :],
        "g_gla_out": g_gla_out[None, :],
        "wo1": w_out[:GLA_V_W].astype(BF16),
        "wo2": w_out[GLA_V_W:].astype(BF16),
        "g_ffn": g_ffn_norm[None, :],
        "wr": wr.astype(BF16),
        "br": br,
        "w_gate": w_gate.astype(BF16),
        "w_up": w_up.astype(BF16),
        "w_down": w_down.astype(BF16),
    }


def _layer(x, positions, s0, gla_chunk, sample_moba, w):
    bsz, l, d = x.shape
    n = bsz * l
    x2d = x.reshape(n, d)
    cos_tab, sin_tab = _rope_tables(positions)
    if sample_moba is None:
        gq, gk, gv, gr, la, qt, kt, vt, k_nat = _inproj(x2d, w, cos_tab, sin_tab, seq_len=l)
        mo = _moba_prompt(qt, k_nat, vt).reshape(n, MOBA_W)
        k_new = kt.reshape(bsz, MOBA_HEADS, MOBA_HD, l).transpose(0, 3, 1, 2)
        v_new = vt.reshape(bsz, MOBA_HEADS, MOBA_HD, l).transpose(0, 3, 1, 2)
    else:
        gq, gk, gv, gr, la, mq, mk, mv = _inproj(x2d, w, cos_tab, sin_tab)
        mk3 = mk.reshape(bsz, l, MOBA_W)
        mv3 = mv.reshape(bsz, l, MOBA_W)
        mo = sample_moba(mq.reshape(bsz, l, MOBA_W), mk3, mv3).reshape(n, MOBA_W)
        k_new = mk3.reshape(bsz, l, MOBA_HEADS, MOBA_HD)
        v_new = mv3.reshape(bsz, l, MOBA_HEADS, MOBA_HD)

    lp = -(-l // gla_chunk) * gla_chunk
    seq = lambda a: jnp.pad(a.reshape(bsz, l, -1), ((0, 0), (0, lp - l), (0, 0)))
    go, s_new = _gla(seq(gq), seq(gk), seq(gv), seq(la), s0.reshape(bsz, GLA_QK_W, GLA_DV), gla_chunk)
    go = go[:, :l].reshape(n, GLA_V_W)

    h, xn, eid, wt, rank, counts = _outproj(x2d, go, gr, mo, w)
    y = _moe(h, xn, eid, wt, rank, counts, w)
    return y.reshape(bsz, l, d), k_new, v_new, s_new.reshape(bsz, GLA_HEADS, GLA_DK, GLA_DV)


def kernel(x_prompt, x_sample, cache_k, cache_v, state_gla, page_table, g_attn_norm, w_in, w_gla_gate_up, b_gla_gate,
           g_gla_out, g_q, g_k, w_out, g_ffn_norm, w_group_router, b_group_router, w_expert_router, b_expert_router,
           w_gate, w_up, w_down):
    depth = w_in.shape[0]
    bsz, l, _ = x_prompt.shape
    dbs, t_new, _ = x_sample.shape
    n_phys, page = cache_k.shape[1], cache_k.shape[2]
    past = page_table.shape[1] * page
    assert l % ROW_TILE == 0 and (dbs * t_new) % ROW_TILE == 0 and ROW_TILE % t_new == 0

    pos_p = np.arange(l)
    pos_s = past + (np.arange(ROW_TILE) % t_new)
    chunk_s = max(SUBLANES, 1 << (t_new - 1).bit_length())
    cache_kt = jnp.transpose(cache_k, (0, 1, 3, 4, 2))
    cache_vt = jnp.transpose(cache_v, (0, 1, 3, 4, 2))

    hp, hs = x_prompt, x_sample
    outs = [[] for _ in range(6)]
    for li in range(depth):
        w = _prep_weights(g_attn_norm[li], w_in[li], w_gla_gate_up[li], b_gla_gate[li], g_gla_out[li], g_q[li],
                          g_k[li], w_out[li], g_ffn_norm[li], w_group_router[li], b_group_router[li],
                          w_expert_router[li], b_expert_router[li], w_gate[li], w_up[li], w_down[li])
        s0 = jnp.zeros((bsz, GLA_HEADS, GLA_DK, GLA_DV), F32)
        hp, kp, vp, sp = _layer(hp, pos_p, s0, min(GLA_CHUNK, l), None, w)

        moba_s = lambda q, k, v, li=li: _moba_sample(q, k, v, cache_kt, cache_vt, li, page_table)
        hs, ks, vs, ss = _layer(hs, pos_s, state_gla[li].astype(F32), chunk_s, moba_s, w)
        for lst, val in zip(outs, (kp, vp, sp.astype(state_gla.dtype), ks, vs, ss.astype(state_gla.dtype))):
            lst.append(val)

    return (hp, hs) + tuple(jnp.stack(o) for o in outs)
```
